```python
import math
import jax
import jax.numpy as jnp
from jax import lax
import numpy as np


D_MODEL = 1024
BATCH = 8
SEQ = 8192
DEPTH = 4

D_MIX = D_MODEL
HEAD_DIM = 64
ATTN_WIDTH = D_MIX // 2
ATTN_HEADS = ATTN_WIDTH // HEAD_DIM
ATTN_KV_HEADS = 2
ATTN_REP = ATTN_HEADS // ATTN_KV_HEADS
ATTN_SCALE = HEAD_DIM ** -0.5
ROPE_DIM = HEAD_DIM // 4
ROPE_THETA = 500000.0
IDX_HEADS = 8
IDX_DIM = HEAD_DIM
IDX_W_SCALE = (IDX_HEADS ** -0.5) * (IDX_DIM ** -0.5)
TOPK_MAX = 256
Q_BLOCK = 128
S5_WIDTH = D_MIX // 4
S5_GROUP_CH = 16
S5_GROUPS = S5_WIDTH // S5_GROUP_CH
S5_STATE = 64
SSD_WIDTH = D_MIX - ATTN_WIDTH - S5_WIDTH
SSD_HEAD_DIM = 64
SSD_HEADS = SSD_WIDTH // SSD_HEAD_DIM
SSD_NGROUPS = 2
SSD_STATE = 64
SSD_CONV = 4
SSD_CHUNK = 128
SSD_XBC = SSD_WIDTH + 2 * SSD_NGROUPS * SSD_STATE
D_FF = 4 * D_MODEL
EPS = 1e-6
IN_SIZES = (ATTN_WIDTH, ATTN_KV_HEADS * HEAD_DIM, ATTN_KV_HEADS * HEAD_DIM,
            IDX_HEADS * IDX_DIM, IDX_DIM, IDX_HEADS,
            S5_WIDTH,
            SSD_WIDTH, SSD_XBC, SSD_HEADS)
N_IN = sum(IN_SIZES)

kernel_name = 'hybrid_dsa_s5_ssd_block'


def _rms(x):
    xf = x.astype(jnp.float32)
    return xf * lax.rsqrt(jnp.mean(xf * xf, axis=-1, keepdims=True) + EPS)


def _rmsnorm(x, g):
    return (_rms(x) * g).astype(x.dtype)


def _split(z, sizes):
    out = []
    start = 0
    for n in sizes:
        out.append(z[..., start:start + n])
        start += n
    return out


def _rope_tables(s):
    pos = jnp.arange(s, dtype=jnp.float32)
    inv_freq = ROPE_THETA ** (-jnp.arange(0, ROPE_DIM, 2, dtype=jnp.float32) / ROPE_DIM)
    ang = pos[:, None] * inv_freq[None, :]
    return jnp.cos(ang), jnp.sin(ang)


def _partial_rope(x, cos, sin):
    half = ROPE_DIM // 2
    x1 = x[..., :half]
    x2 = x[..., half:ROPE_DIM]
    rot = jnp.concatenate([x1 * cos - x2 * sin, x2 * cos + x1 * sin,
                           x[..., ROPE_DIM:].astype(jnp.float32)], axis=-1)
    return rot.astype(x.dtype)


def _gather_rows(table, idx):
    return jax.vmap(lambda t, i: t[i])(table, idx)


def _dsa_mixer(q, k, v, qi, ki, wi, q_g, k_g, ki_g, cos, sin):
    b, s, _ = q.shape
    cos_h, sin_h = cos[:, None, :], sin[:, None, :]
    q = _partial_rope(_rmsnorm(q.reshape(b, s, ATTN_HEADS, HEAD_DIM), q_g), cos_h, sin_h)
    k = _partial_rope(_rmsnorm(k.reshape(b, s, ATTN_KV_HEADS, HEAD_DIM), k_g), cos_h, sin_h)
    v = v.reshape(b, s, ATTN_KV_HEADS, HEAD_DIM)
    qi = _partial_rope(qi.reshape(b, s, IDX_HEADS, IDX_DIM), cos_h, sin_h)
    ki = _partial_rope(_rmsnorm(ki, ki_g), cos, sin)
    wi = wi.astype(jnp.float32) * IDX_W_SCALE
    topk = min(TOPK_MAX, s // 4)
    nb = s // Q_BLOCK
    key_pos = jnp.arange(s, dtype=jnp.int32)

    def blocks(a):
        return a.reshape((b, nb, Q_BLOCK) + a.shape[2:]).swapaxes(0, 1)

    def attend(args):
        q_b, qi_b, wi_b, start = args
        q_pos = start + jnp.arange(Q_BLOCK, dtype=jnp.int32)
        idx_logits = jnp.einsum('bqhd,bsd->bqhs', qi_b, ki).astype(jnp.float32)
        score = jnp.einsum('bqh,bqhs->bqs', wi_b, jax.nn.relu(idx_logits))
        admissible = key_pos[None, :] <= q_pos[:, None]
        score = jnp.where(admissible[None], score, -jnp.inf)
        _, sel = lax.top_k(score, topk)
        valid = sel <= q_pos[None, :, None]
        k_sel = _gather_rows(k, sel)
        v_sel = _gather_rows(v, sel)
        qg = q_b.reshape(b, Q_BLOCK, ATTN_KV_HEADS, ATTN_REP, HEAD_DIM)
        logits = jnp.einsum('bqgrd,bqkgd->bqgrk', qg, k_sel).astype(jnp.float32) * ATTN_SCALE
        logits = jnp.where(valid[:, :, None, None, :], logits, -jnp.inf)
        p = jax.nn.softmax(logits, axis=-1).astype(v_sel.dtype)
        o = jnp.einsum('bqgrk,bqkgd->bqgrd', p, v_sel)
        return o.reshape(b, Q_BLOCK, ATTN_WIDTH)

    starts = jnp.arange(nb, dtype=jnp.int32) * Q_BLOCK
    out = lax.map(attend, (blocks(q), blocks(qi), blocks(wi), starts))
    return out.swapaxes(0, 1).reshape(b, s, ATTN_WIDTH)


def _complex_affine_combine(e1, e2):
    a1r, a1i, b1r, b1i = e1
    a2r, a2i, b2r, b2i = e2
    ar = a2r * a1r - a2i * a1i
    ai = a2r * a1i + a2i * a1r
    br = a2r * b1r - a2i * b1i + b2r
    bi = a2r * b1i + a2i * b1r + b2i
    return (ar, ai, br, bi)


def _s5_mixer(u, lam_re, lam_im, log_step, b_re, b_im, c_re, c_im, d_skip, glu_w, glu_b):
    b, s, _ = u.shape
    uf = u.astype(jnp.float32).reshape(b, s, S5_GROUPS, S5_GROUP_CH)
    step = jnp.exp(log_step.astype(jnp.float32))[:, None]
    lr = lam_re.astype(jnp.float32)
    li = lam_im.astype(jnp.float32)
    mag = jnp.exp(lr * step)
    ab_re = mag * jnp.cos(li * step)
    ab_im = mag * jnp.sin(li * step)
    den = lr * lr + li * li
    cr = ((ab_re - 1.0) * lr + ab_im * li) / den
    ci = (ab_im * lr - (ab_re - 1.0) * li) / den
    bb_re = cr[..., None] * b_re - ci[..., None] * b_im
    bb_im = cr[..., None] * b_im + ci[..., None] * b_re
    bu_re = jnp.einsum('bsgh,gph->bsgp', uf, bb_re)
    bu_im = jnp.einsum('bsgh,gph->bsgp', uf, bb_im)
    a_re = jnp.broadcast_to(ab_re, bu_re.shape)
    a_im = jnp.broadcast_to(ab_im, bu_re.shape)
    _, _, xr, xi = lax.associative_scan(_complex_affine_combine, (a_re, a_im, bu_re, bu_im), axis=1)
    y = jnp.einsum('bsgp,ghp->bsgh', xr, c_re) - jnp.einsum('bsgp,ghp->bsgh', xi, c_im)
    y = y.reshape(b, s, S5_WIDTH) + d_skip * u.astype(jnp.float32)
    y = jax.nn.gelu(y)
    return (y * jax.nn.sigmoid(y @ glu_w + glu_b)).astype(u.dtype)


def _causal_dwconv(x, w, bias):
    out = lax.conv_general_dilated(x, w[:, None, :].astype(x.dtype), window_strides=(1,),
                                   padding=((SSD_CONV - 1, 0),),
                                   dimension_numbers=('NWC', 'WIO', 'NWC'),
                                   feature_group_count=x.shape[-1])
    return out + bias


def _ssd_chunked(x, a_dt, bmat, cmat):
    b, s, h, p = x.shape
    n = bmat.shape[-1]
    c, l = s // SSD_CHUNK, SSD_CHUNK
    x = x.reshape(b, c, l, h, p)
    bmat = bmat.reshape(b, c, l, h, n)
    cmat = cmat.reshape(b, c, l, h, n)
    a = a_dt.reshape(b, c, l, h).transpose(0, 3, 1, 2)
    a_cs = jnp.cumsum(a, axis=-1)
    seg = a_cs[..., :, None] - a_cs[..., None, :]
    causal = jnp.tril(jnp.ones((l, l), dtype=bool))
    decay = jnp.exp(jnp.where(causal, seg, -jnp.inf))
    scores = jnp.einsum('bclhn,bcshn->bhcls', cmat, bmat) * decay
    y_diag = jnp.einsum('bhcls,bcshp->bclhp', scores, x)
    decay_states = jnp.exp(a_cs[..., -1:] - a_cs)
    states = jnp.einsum('bclhn,bhcl,bclhp->bchpn', bmat, decay_states, x)
    chunk_decay = jnp.exp(a_cs[..., -1])

    def step(carry, inp):
        st, dec = inp
        return carry * dec[:, :, None, None] + st, carry

    init = jnp.zeros((b, h, p, n), dtype=x.dtype)
    _, prev = lax.scan(step, init, (states.transpose(1, 0, 2, 3, 4), chunk_decay.transpose(2, 0, 1)))
    prev = prev.transpose(1, 0, 2, 3, 4)
    y_off = jnp.einsum('bclhn,bchpn,bhcl->bclhp', cmat, prev, jnp.exp(a_cs))
    return (y_diag + y_off).reshape(b, s, h, p)


def _ssd_mixer(z, xbc, dt_raw, conv_w, conv_b, dt_bias, a_log, d_skip, norm_g):
    b, s, _ = xbc.shape
    xbc = jax.nn.silu(_causal_dwconv(xbc, conv_w, conv_b))
    xs, bm, cm = _split(xbc, (SSD_WIDTH, SSD_NGROUPS * SSD_STATE, SSD_NGROUPS * SSD_STATE))
    xs = xs.reshape(b, s, SSD_HEADS, SSD_HEAD_DIM).astype(jnp.float32)
    rep = SSD_HEADS // SSD_NGROUPS
    bm = jnp.repeat(bm.reshape(b, s, SSD_NGROUPS, SSD_STATE).astype(jnp.float32), rep, axis=2)
    cm = jnp.repeat(cm.reshape(b, s, SSD_NGROUPS, SSD_STATE).astype(jnp.float32), rep, axis=2)
    dt = jax.nn.softplus((dt_raw + dt_bias).astype(jnp.float32))
    a = -jnp.exp(a_log.astype(jnp.float32))
    y = _ssd_chunked(xs * dt[..., None], dt * a, bm, cm)
    y = y + d_skip[:, None] * xs
    y = y.reshape(b, s, SSD_WIDTH) * jax.nn.silu(z.astype(jnp.float32))
    y = _rms(y.reshape(b, s, SSD_NGROUPS, SSD_WIDTH // SSD_NGROUPS)).reshape(b, s, SSD_WIDTH)
    return (y * norm_g).astype(z.dtype)


def setup_inputs(seed: int = 0) -> dict:
    key = jax.random.key(seed)
    ks = jax.random.split(key, 26)
    f32 = jnp.float32

    def nrm(k, shape, scale):
        return jax.random.normal(k, shape, f32) * scale

    def gain(k, shape):
        return 1.0 + 0.02 * jax.random.normal(k, shape, f32)

    lam_im = jnp.broadcast_to(jnp.pi * jnp.arange(S5_STATE, dtype=f32), (DEPTH, S5_GROUPS, S5_STATE))
    ssd_dt = jnp.exp(jax.random.uniform(ks[18], (DEPTH, SSD_HEADS), f32, math.log(0.001), math.log(0.1)))
    return {
        'x': nrm(ks[0], (BATCH, SEQ, D_MODEL), 1.0),
        'norm_mix_g': gain(ks[1], (DEPTH, D_MODEL)),
        'w_in': nrm(ks[2], (DEPTH, D_MODEL, N_IN), D_MODEL ** -0.5),
        'attn_q_norm_g': gain(ks[3], (DEPTH, HEAD_DIM)),
        'attn_k_norm_g': gain(ks[4], (DEPTH, HEAD_DIM)),
        'idx_k_norm_g': gain(ks[5], (DEPTH, IDX_DIM)),
        's5_lambda_re': -0.5 + 0.01 * jax.random.normal(ks[6], (DEPTH, S5_GROUPS, S5_STATE), f32),
        's5_lambda_im': lam_im + 0.01 * jax.random.normal(ks[7], (DEPTH, S5_GROUPS, S5_STATE), f32),
        's5_log_step': jax.random.uniform(ks[8], (DEPTH, S5_GROUPS), f32, math.log(0.001), math.log(0.1)),
        's5_b_re': nrm(ks[9], (DEPTH, S5_GROUPS, S5_STATE, S5_GROUP_CH), (2 * S5_GROUP_CH) ** -0.5),
        's5_b_im': nrm(ks[10], (DEPTH, S5_GROUPS, S5_STATE, S5_GROUP_CH), (2 * S5_GROUP_CH) ** -0.5),
        's5_c_re': nrm(ks[11], (DEPTH, S5_GROUPS, S5_GROUP_CH, S5_STATE), S5_STATE ** -0.5),
        's5_c_im': nrm(ks[12], (DEPTH, S5_GROUPS, S5_GROUP_CH, S5_STATE), S5_STATE ** -0.5),
        's5_d': nrm(ks[13], (DEPTH, S5_WIDTH), 1.0),
        's5_glu_w': nrm(ks[14], (DEPTH, S5_WIDTH, S5_WIDTH), S5_WIDTH ** -0.5),
        's5_glu_b': nrm(ks[15], (DEPTH, S5_WIDTH), 0.01),
        'ssd_conv_w': nrm(ks[16], (DEPTH, SSD_CONV, SSD_XBC), SSD_CONV ** -0.5),
        'ssd_conv_b': nrm(ks[17], (DEPTH, SSD_XBC), 0.01),
        'ssd_dt_bias': ssd_dt + jnp.log(-jnp.expm1(-ssd_dt)),
        'ssd_a_log': jnp.log(jax.random.uniform(ks[19], (DEPTH, SSD_HEADS), f32, 1.0, 16.0)),
        'ssd_d': 1.0 + 0.01 * jax.random.normal(ks[20], (DEPTH, SSD_HEADS), f32),
        'ssd_norm_g': gain(ks[21], (DEPTH, SSD_WIDTH)),
        'w_out': nrm(ks[22], (DEPTH, D_MIX, D_MODEL), D_MIX ** -0.5),
        'norm_mlp_g': gain(ks[23], (DEPTH, D_MODEL)),
        'w_up': nrm(ks[24], (DEPTH, D_MODEL, D_FF), D_MODEL ** -0.5),
        'w_down': nrm(ks[25], (DEPTH, D_FF, D_MODEL), D_FF ** -0.5),
    }


def reference(x, norm_mix_g, w_in, attn_q_norm_g, attn_k_norm_g, idx_k_norm_g,
              s5_lambda_re, s5_lambda_im, s5_log_step, s5_b_re, s5_b_im, s5_c_re, s5_c_im,
              s5_d, s5_glu_w, s5_glu_b,
              ssd_conv_w, ssd_conv_b, ssd_dt_bias, ssd_a_log, ssd_d, ssd_norm_g,
              w_out, norm_mlp_g, w_up, w_down):
    cos, sin = _rope_tables(x.shape[1])
    for l in range(DEPTH):
        h = _rmsnorm(x, norm_mix_g[l])
        z = h @ w_in[l]
        q, k, v, qi, ki, wi, u5, z_ssd, xbc, dt_raw = _split(z, IN_SIZES)
        attn_out = _dsa_mixer(q, k, v, qi, ki, wi, attn_q_norm_g[l], attn_k_norm_g[l],
                              idx_k_norm_g[l], cos, sin)
        s5_out = _s5_mixer(u5, s5_lambda_re[l], s5_lambda_im[l], s5_log_step[l],
                           s5_b_re[l], s5_b_im[l], s5_c_re[l], s5_c_im[l],
                           s5_d[l], s5_glu_w[l], s5_glu_b[l])
        ssd_out = _ssd_mixer(z_ssd, xbc, dt_raw, ssd_conv_w[l], ssd_conv_b[l], ssd_dt_bias[l],
                             ssd_a_log[l], ssd_d[l], ssd_norm_g[l])
        mix = jnp.concatenate([attn_out, s5_out, ssd_out], axis=-1) @ w_out[l]
        x = x + mix.astype(x.dtype)
        h2 = _rmsnorm(x, norm_mlp_g[l])
        x = x + ((jax.nn.relu(h2 @ w_up[l]) ** 2) @ w_down[l]).astype(x.dtype)
    return x
```

```python
import functools
import math

import jax
import jax.numpy as jnp
import numpy as np
from jax import lax
from jax.experimental import pallas as pl
from jax.experimental.pallas import tpu as pltpu

F32 = jnp.float32
BF16 = jnp.bfloat16

HEAD_DIM = 64
ATTN_HEADS = 8
ATTN_KV_HEADS = 2
ATTN_REP = ATTN_HEADS // ATTN_KV_HEADS
ATTN_WIDTH = ATTN_HEADS * HEAD_DIM
KV_WIDTH = ATTN_KV_HEADS * HEAD_DIM
ATTN_SCALE = HEAD_DIM ** -0.5
ROPE_DIM = HEAD_DIM // 4
ROPE_HALF = ROPE_DIM // 2
ROPE_THETA = 500000.0
IDX_HEADS = 8
IDX_DIM = HEAD_DIM
IDX_W_SCALE = (IDX_HEADS ** -0.5) * (IDX_DIM ** -0.5)
TOPK_MAX = 256
S5_WIDTH = 256
S5_GROUP_CH = 16
S5_GROUPS = S5_WIDTH // S5_GROUP_CH
S5_STATE = 64
S5_LANES = S5_GROUPS * S5_STATE
SSD_WIDTH = 256
SSD_HEAD_DIM = 64
SSD_HEADS = SSD_WIDTH // SSD_HEAD_DIM
SSD_NGROUPS = 2
SSD_STATE = 64
SSD_CONV = 4
SSD_XBC = SSD_WIDTH + 2 * SSD_NGROUPS * SSD_STATE
EPS = 1e-6
LANES = 128
SUBLANES = 8

ATTN_SLAB = 1536
OFF_Q, OFF_K, OFF_V, OFF_QI, OFF_KW = 0, 512, 640, 768, 1280
OFF_S5 = ATTN_SLAB
SSD_SLAB = 896
OFF_SSD = OFF_S5 + S5_WIDTH
N_IN_PAD = OFF_SSD + SSD_SLAB

KEY_LO = -2139095040
KEY_HI = 2139095041
NEG_BIG = -1e30
VMEM_LIMIT = 56 * 1024 * 1024


def _cparams(*sem):
    return pltpu.CompilerParams(dimension_semantics=sem, vmem_limit_bytes=VMEM_LIMIT)


def _const_spec(shape):
    nd = len(shape)
    return pl.BlockSpec(shape, lambda *_: (0,) * nd)


def _norm_matmul_kernel(x_ref, g_ref, w_ref, o_ref):
    x = x_ref[...]
    ms = jnp.mean(x * x, axis=-1, keepdims=True)
    h = (x * lax.rsqrt(ms + EPS) * g_ref[...]).astype(BF16)
    o_ref[...] = jnp.dot(h, w_ref[...], preferred_element_type=F32)


def _norm_matmul(x2d, g, w, tm):
    n, d = x2d.shape
    npad = w.shape[1]
    return pl.pallas_call(
        _norm_matmul_kernel,
        grid=(n // tm,),
        in_specs=[pl.BlockSpec((tm, d), lambda i: (i, 0)),
                  _const_spec((1, d)),
                  _const_spec((d, npad))],
        out_specs=pl.BlockSpec((tm, npad), lambda i: (i, 0)),
        out_shape=jax.ShapeDtypeStruct((n, npad), F32),
        compiler_params=_cparams("parallel"),
        name="norm_in_proj",
    )(x2d, g, w)


def _segment_mean(sq, bd_ref):
    hi = sq.astype(BF16)
    lo = (sq - hi.astype(F32)).astype(BF16)
    bd = bd_ref[...]
    return (jnp.dot(hi, bd, preferred_element_type=F32)
            + jnp.dot(lo, bd, preferred_element_type=F32))


def _rope(x, c, sa, sb):
    w = x.shape[-1]
    reps = w // HEAD_DIM
    c, sa, sb = (jnp.tile(t, (1, reps)) if reps > 1 else t for t in (c, sa, sb))
    return x * c + pltpu.roll(x, w - ROPE_HALF, 1) * sa + pltpu.roll(x, ROPE_HALF, 1) * sb


def _prep_kernel(z_ref, c_ref, sa_ref, sb_ref, qg_ref, kg_ref, kig_ref,
                 bdq_ref, bdk_ref, bdi_ref,
                 q_out, k_out, v_out, qi_out, ki_out, w_out):
    z = z_ref[0]
    c, sa, sb = c_ref[...], sa_ref[...], sb_ref[...]
    q = z[:, OFF_Q:OFF_Q + ATTN_WIDTH]
    qn = q * lax.rsqrt(_segment_mean(q * q, bdq_ref) + EPS) * qg_ref[...]
    qn = _rope(qn, c, sa, sb) * ATTN_SCALE
    for h in range(ATTN_HEADS):
        q_out[0, h] = qn[:, h * HEAD_DIM:(h + 1) * HEAD_DIM].astype(BF16)
    k = z[:, OFF_K:OFF_K + KV_WIDTH]
    kn = k * lax.rsqrt(_segment_mean(k * k, bdk_ref) + EPS) * kg_ref[...]
    kn = _rope(kn, c, sa, sb)
    v = z[:, OFF_V:OFF_V + KV_WIDTH]
    for h in range(ATTN_KV_HEADS):
        k_out[0, h] = kn[:, h * HEAD_DIM:(h + 1) * HEAD_DIM].astype(BF16)
        v_out[0, h] = v[:, h * HEAD_DIM:(h + 1) * HEAD_DIM].astype(BF16)
    qi = _rope(z[:, OFF_QI:OFF_QI + IDX_HEADS * IDX_DIM], c, sa, sb)
    for h in range(IDX_HEADS):
        qi_out[0, h] = qi[:, h * IDX_DIM:(h + 1) * IDX_DIM].astype(BF16)
    kw = z[:, OFF_KW:OFF_KW + LANES]
    kin = kw * lax.rsqrt(_segment_mean(kw * kw, bdi_ref) + EPS) * kig_ref[...]
    kin = _rope(kin, c, sa, sb)
    ki_out[0] = kin[:, 0:IDX_DIM].astype(BF16)
    w_out[0] = kw[:, IDX_DIM:IDX_DIM + IDX_HEADS] * IDX_W_SCALE


def _prep(z3, tables, qg, kg, kig, bdq, bdk, bdi, tt):
    b, s, _ = z3.shape
    tab_spec = pl.BlockSpec((tt, HEAD_DIM), lambda bi, j: (j, 0))

    def head_spec(nh):
        return pl.BlockSpec((1, nh, tt, HEAD_DIM), lambda bi, j: (bi, 0, j, 0))

    return pl.pallas_call(
        _prep_kernel,
        grid=(b, s // tt),
        in_specs=[pl.BlockSpec((1, tt, ATTN_SLAB), lambda bi, j: (bi, j, 0)),
                  tab_spec, tab_spec, tab_spec,
                  _const_spec(qg.shape), _const_spec(kg.shape), _const_spec(kig.shape),
                  _const_spec(bdq.shape), _const_spec(bdk.shape), _const_spec(bdi.shape)],
        out_specs=[head_spec(ATTN_HEADS), head_spec(ATTN_KV_HEADS), head_spec(ATTN_KV_HEADS),
                   head_spec(IDX_HEADS),
                   pl.BlockSpec((1, tt, IDX_DIM), lambda bi, j: (bi, j, 0)),
                   pl.BlockSpec((1, tt, IDX_HEADS), lambda bi, j: (bi, j, 0))],
        out_shape=[jax.ShapeDtypeStruct((b, ATTN_HEADS, s, HEAD_DIM), BF16),
                   jax.ShapeDtypeStruct((b, ATTN_KV_HEADS, s, HEAD_DIM), BF16),
                   jax.ShapeDtypeStruct((b, ATTN_KV_HEADS, s, HEAD_DIM), BF16),
                   jax.ShapeDtypeStruct((b, IDX_HEADS, s, IDX_DIM), BF16),
                   jax.ShapeDtypeStruct((b, s, IDX_DIM), BF16),
                   jax.ShapeDtypeStruct((b, s, IDX_HEADS), F32)],
        compiler_params=_cparams("parallel", "parallel"),
        name="attn_prep",
    )(z3, *tables, qg, kg, kig, bdq, bdk, bdi)


_NT = (((1,), (1,)), ((), ()))


def _key_to_f32(key):
    bits = key ^ (lax.shift_right_arithmetic(key, 31) & 0x7FFFFFFF)
    return lax.bitcast_convert_type(bits, F32)


def _dsa_kernel(qi_ref, w_ref, q_ref, ki_ref, k_ref, v_ref, o_ref,
                sc_ref, m_ref, l_ref, acc_ref, *, tq, ck, topk):
    qb = pl.program_id(1)
    n_chunks = lax.div(qb * tq, jnp.int32(ck)) + 1
    sub = ck // LANES
    row_pos = qb * tq + lax.broadcasted_iota(jnp.int32, (tq, 1), 0)
    lane_ids = lax.broadcasted_iota(jnp.int32, (1, ck), 1)

    qi = qi_ref[0].reshape(IDX_HEADS * tq, IDX_DIM)
    w = w_ref[0]

    def score_chunk(c, carry):
        off = pl.multiple_of(c * ck, ck)
        kic = ki_ref[0, pl.ds(off, ck), :]
        lg = lax.dot_general(qi, kic, _NT, preferred_element_type=F32)
        sc = jnp.zeros((tq, ck), F32)
        for h in range(IDX_HEADS):
            sc = sc + w[:, h:h + 1] * jnp.maximum(lg[h * tq:(h + 1) * tq], 0.0)
        sc_ref[c] = jnp.where(off + lane_ids <= row_pos, sc, -jnp.inf)
        return carry

    lax.fori_loop(0, n_chunks, score_chunk, 0)

    def count(pred):
        def body(c, acc):
            x = sc_ref[c]
            for j in range(sub):
                acc = acc + pred(x[:, j * LANES:(j + 1) * LANES], c * ck + j * LANES).astype(jnp.int32)
            return acc
        acc = lax.fori_loop(0, n_chunks, body, jnp.zeros((tq, LANES), jnp.int32))
        return jnp.sum(acc, axis=-1, keepdims=True)

    def bisect_value(_, carry):
        lo, hi, c_lo, c_hi = carry
        mid = lo + lax.shift_right_logical(hi - lo, 1)
        thr = jnp.broadcast_to(_key_to_f32(mid), (tq, LANES))
        cnt = count(lambda x, base: x >= thr)
        ge = cnt >= topk
        return (jnp.where(ge, mid, lo), jnp.where(ge, hi, mid),
                jnp.where(ge, cnt, c_lo), jnp.where(ge, c_hi, cnt))

    init = (jnp.full((tq, 1), KEY_LO, jnp.int32), jnp.full((tq, 1), KEY_HI, jnp.int32),
            jnp.full((tq, 1), topk, jnp.int32), jnp.zeros((tq, 1), jnp.int32))
    lo, _, c_ge, c_gt = lax.fori_loop(0, 32, bisect_value, init)
    thr = _key_to_f32(lo)

    need = c_ge > topk

    @pl.when(jnp.max(need.astype(jnp.int32)) > 0)
    def _():
        want = topk - c_gt
        thr_b = jnp.broadcast_to(thr, (tq, LANES))
        lane = lax.broadcasted_iota(jnp.int32, (tq, LANES), 1)

        def bisect_index(_, carry):
            jlo, jhi = carry
            mid = jlo + lax.shift_right_logical(jhi - jlo, 1)
            mid_b = jnp.broadcast_to(mid, (tq, LANES))
            cnt = count(lambda x, base: (x == thr_b) & (lane + base <= mid_b))
            ge = cnt >= want
            return jnp.where(ge, jlo, mid), jnp.where(ge, mid, jhi)

        n_idx_steps = int(math.ceil(math.log2(sc_ref.shape[0] * ck + 1)))
        jinit = (jnp.full((tq, 1), -1, jnp.int32), (n_chunks * ck - 1) + jnp.zeros((tq, 1), jnp.int32))
        _, cut = lax.fori_loop(0, n_idx_steps, bisect_index, jinit)
        cut = jnp.where(need, cut, jnp.int32(2 ** 30))

        def drop(c, carry):
            x = sc_ref[c]
            sc_ref[c] = jnp.where((x == thr) & (c * ck + lane_ids > cut), -jnp.inf, x)
            return carry

        lax.fori_loop(0, n_chunks, drop, 0)

    m_ref[...] = jnp.full(m_ref.shape, NEG_BIG, F32)
    l_ref[...] = jnp.zeros(l_ref.shape, F32)
    acc_ref[...] = jnp.zeros(acc_ref.shape, F32)
    rows = ATTN_REP * tq

    def attend_chunk(c, carry):
        off = pl.multiple_of(c * ck, ck)
        bias = jnp.where(sc_ref[c] >= thr, 0.0, NEG_BIG)
        for g in range(ATTN_KV_HEADS):
            qg = q_ref[0, g * ATTN_REP:(g + 1) * ATTN_REP].reshape(rows, HEAD_DIM)
            kc = k_ref[0, g, pl.ds(off, ck), :]
            vc = v_ref[0, g, pl.ds(off, ck), :]
            s = lax.dot_general(qg, kc, _NT, preferred_element_type=F32)
            s = (s.reshape(ATTN_REP, tq, ck) + bias[None]).reshape(rows, ck)
            m_old = m_ref[g]
            m_new = jnp.maximum(m_old, jnp.max(s, axis=-1, keepdims=True))
            alpha = jnp.exp(m_old - m_new)
            p = jnp.exp(s - m_new)
            l_ref[g] = alpha * l_ref[g] + jnp.sum(p, axis=-1, keepdims=True)
            acc_ref[g] = alpha * acc_ref[g] + jnp.dot(p.astype(BF16), vc, preferred_element_type=F32)
            m_ref[g] = m_new
        return carry

    lax.fori_loop(0, n_chunks, attend_chunk, 0)

    for g in range(ATTN_KV_HEADS):
        o = acc_ref[g] / l_ref[g]
        for r in range(ATTN_REP):
            h = g * ATTN_REP + r
            o_ref[0, :, h * HEAD_DIM:(h + 1) * HEAD_DIM] = o[r * tq:(r + 1) * tq].astype(o_ref.dtype)


def _dsa(qh, kh, vh, qih, kir, w, tq, ck):
    b, _, s, _ = qh.shape
    topk = min(TOPK_MAX, s // 4)
    kernel = functools.partial(_dsa_kernel, tq=tq, ck=ck, topk=topk)
    rows = ATTN_REP * tq
    return pl.pallas_call(
        kernel,
        grid=(b, s // tq),
        in_specs=[pl.BlockSpec((1, IDX_HEADS, tq, IDX_DIM), lambda bi, j: (bi, 0, j, 0)),
                  pl.BlockSpec((1, tq, IDX_HEADS), lambda bi, j: (bi, j, 0)),
                  pl.BlockSpec((1, ATTN_HEADS, tq, HEAD_DIM), lambda bi, j: (bi, 0, j, 0)),
                  pl.BlockSpec((1, s, IDX_DIM), lambda bi, j: (bi, 0, 0)),
                  pl.BlockSpec((1, ATTN_KV_HEADS, s, HEAD_DIM), lambda bi, j: (bi, 0, 0, 0)),
                  pl.BlockSpec((1, ATTN_KV_HEADS, s, HEAD_DIM), lambda bi, j: (bi, 0, 0, 0))],
        out_specs=pl.BlockSpec((1, tq, ATTN_WIDTH), lambda bi, j: (bi, j, 0)),
        out_shape=jax.ShapeDtypeStruct((b, s, ATTN_WIDTH), BF16),
        scratch_shapes=[pltpu.VMEM((s // ck, tq, ck), F32),
                        pltpu.VMEM((ATTN_KV_HEADS, rows, 1), F32),
                        pltpu.VMEM((ATTN_KV_HEADS, rows, 1), F32),
                        pltpu.VMEM((ATTN_KV_HEADS, rows, HEAD_DIM), F32)],
        compiler_params=_cparams("parallel", "arbitrary"),
        name="dsa_mixer",
    )(qih, w, qh, kir, kh, vh)


def _gelu_tanh(y):
    return 0.5 * y * (1.0 + jnp.tanh(math.sqrt(2.0 / math.pi) * (y + 0.044715 * (y * y * y))))


def _s5_kernel(u_ref, wb_ref, ar_ref, ai_ref, wc_ref, d_ref, gw_ref, gb_ref, o_ref,
               x_ref, st_ref, *, tt, nb):
    @pl.when(pl.program_id(0) == 0)
    def _():
        st_ref[...] = jnp.zeros(st_ref.shape, F32)

    u = u_ref[...]
    x_ref[...] = jnp.dot(u.astype(BF16), wb_ref[...], preferred_element_type=F32)
    ar = jnp.broadcast_to(ar_ref[...], (nb, S5_LANES))
    ai = jnp.broadcast_to(ai_ref[...], (nb, S5_LANES))

    def step(t, carry):
        xr, xi = carry
        r0 = pl.multiple_of(t * nb, nb)
        nxr = ar * xr - ai * xi + x_ref[pl.ds(r0, nb), 0:S5_LANES]
        nxi = ar * xi + ai * xr + x_ref[pl.ds(r0, nb), S5_LANES:2 * S5_LANES]
        x_ref[pl.ds(r0, nb), 0:S5_LANES] = nxr
        x_ref[pl.ds(r0, nb), S5_LANES:2 * S5_LANES] = nxi
        return nxr, nxi

    xr, xi = lax.fori_loop(0, tt, step, (st_ref[0:nb, :], st_ref[nb:2 * nb, :]), unroll=4)
    st_ref[0:nb, :] = xr
    st_ref[nb:2 * nb, :] = xi

    y = jnp.dot(x_ref[...].astype(BF16), wc_ref[...], preferred_element_type=F32)
    y = _gelu_tanh(y + d_ref[...] * u)
    gate = jnp.dot(y.astype(BF16), gw_ref[...], preferred_element_type=F32) + gb_ref[...]
    o_ref[...] = (y * jax.nn.sigmoid(gate)).astype(o_ref.dtype)


def _s5(u_t, wb, ar, ai, wc, d, gw, gb, nb, tt):
    n = u_t.shape[0]
    rows = nb * tt
    kernel = functools.partial(_s5_kernel, tt=tt, nb=nb)
    return pl.pallas_call(
        kernel,
        grid=(n // rows,),
        in_specs=[pl.BlockSpec((rows, S5_WIDTH), lambda i: (i, 0)),
                  _const_spec(wb.shape), _const_spec(ar.shape), _const_spec(ai.shape),
                  _const_spec(wc.shape), _const_spec(d.shape), _const_spec(gw.shape),
                  _const_spec(gb.shape)],
        out_specs=pl.BlockSpec((rows, S5_WIDTH), lambda i: (i, 0)),
        out_shape=jax.ShapeDtypeStruct((n, S5_WIDTH), BF16),
        scratch_shapes=[pltpu.VMEM((rows, 2 * S5_LANES), F32),
                        pltpu.VMEM((2 * nb, S5_LANES), F32)],
        compiler_params=_cparams("arbitrary"),
        name="s5_mixer",
    )(u_t, wb, ar, ai, wc, d, gw, gb)


_TN = (((0,), (0,)), ((), ()))
_HI = lax.Precision.HIGHEST


def _ssd_kernel(z_ref, cw_ref, cb_ref, dtb_ref, a_ref, dsk_ref, ng_ref, tri_ref, trit_ref,
                o_ref, ext_ref, st_ref, *, cl):
    @pl.when(pl.program_id(1) == 0)
    def _():
        st_ref[...] = jnp.zeros(st_ref.shape, F32)
        ext_ref[0:SUBLANES, :] = jnp.zeros((SUBLANES, SSD_XBC), F32)

    tile = z_ref[0]
    zg = tile[:, 0:SSD_WIDTH]
    ext_ref[SUBLANES:SUBLANES + cl, :] = tile[:, SSD_WIDTH:SSD_WIDTH + SSD_XBC]
    conv = cb_ref[...]
    for k in range(SSD_CONV):
        start = SUBLANES - (SSD_CONV - 1) + k
        conv = conv + cw_ref[k:k + 1, :] * ext_ref[start:start + cl, :]
    ext_ref[0:SUBLANES, :] = ext_ref[cl:cl + SUBLANES, :]
    xa = conv * jax.nn.sigmoid(conv)
    xs = xa[:, 0:SSD_WIDTH]
    bm = xa[:, SSD_WIDTH:SSD_WIDTH + SSD_NGROUPS * SSD_STATE]
    cm = xa[:, SSD_WIDTH + SSD_NGROUPS * SSD_STATE:SSD_XBC]

    dtx = tile[:, SSD_WIDTH + SSD_XBC:SSD_SLAB] + dtb_ref[...]
    dt = jnp.maximum(dtx, 0.0) + jnp.log1p(jnp.exp(-jnp.abs(dtx)))
    a_dt = dt * a_ref[...]
    acs = jnp.dot(tri_ref[...], a_dt, precision=_HI, preferred_element_type=F32)
    acs_row = jnp.dot(a_dt.T[0:SUBLANES], trit_ref[...], precision=_HI, preferred_element_type=F32)
    causal = (lax.broadcasted_iota(jnp.int32, (cl, cl), 0)
              >= lax.broadcasted_iota(jnp.int32, (cl, cl), 1))

    cb_scores = []
    for g in range(SSD_NGROUPS):
        bg = bm[:, g * SSD_STATE:(g + 1) * SSD_STATE].astype(BF16)
        cg = cm[:, g * SSD_STATE:(g + 1) * SSD_STATE].astype(BF16)
        cb_scores.append(lax.dot_general(cg, bg, _NT, preferred_element_type=F32))

    rep = SSD_HEADS // SSD_NGROUPS
    ys = []
    for h in range(SSD_HEADS):
        g = h // rep
        acol = acs[:, h:h + 1]
        arow = acs_row[h:h + 1, :]
        alast = acs[cl - 1:cl, h:h + 1]
        decay = jnp.exp(jnp.where(causal, acol - arow, -jnp.inf))
        xs_h = xs[:, h * SSD_HEAD_DIM:(h + 1) * SSD_HEAD_DIM]
        xdt = (xs_h * dt[:, h:h + 1]).astype(BF16)
        y = jnp.dot((cb_scores[g] * decay).astype(BF16), xdt, preferred_element_type=F32)
        bg = bm[:, g * SSD_STATE:(g + 1) * SSD_STATE]
        cg = cm[:, g * SSD_STATE:(g + 1) * SSD_STATE].astype(BF16)
        prev = st_ref[h]
        y = y + jnp.dot(cg, prev.astype(BF16), preferred_element_type=F32) * jnp.exp(acol)
        bdec = (bg * jnp.exp(alast - acol)).astype(BF16)
        st_ref[h] = prev * jnp.exp(alast) + lax.dot_general(bdec, xdt, _TN, preferred_element_type=F32)
        ys.append(y + dsk_ref[:, h:h + 1] * xs_h)

    y = jnp.concatenate(ys, axis=-1) * (zg * jax.nn.sigmoid(zg))
    gw = SSD_WIDTH // SSD_NGROUPS
    outs = []
    for g in range(SSD_NGROUPS):
        yg = y[:, g * gw:(g + 1) * gw]
        outs.append(yg * lax.rsqrt(jnp.mean(yg * yg, axis=-1, keepdims=True) + EPS))
    o_ref[0] = (jnp.concatenate(outs, axis=-1) * ng_ref[...]).astype(o_ref.dtype)


def _ssd(z3, cw, cb, dtb, a, dsk, ng, tri, trit, cl):
    b, s, _ = z3.shape
    kernel = functools.partial(_ssd_kernel, cl=cl)
    return pl.pallas_call(
        kernel,
        grid=(b, s // cl),
        in_specs=[pl.BlockSpec((1, cl, SSD_SLAB), lambda bi, j: (bi, j, OFF_SSD // SSD_SLAB)),
                  _const_spec(cw.shape), _const_spec(cb.shape), _const_spec(dtb.shape),
                  _const_spec(a.shape), _const_spec(dsk.shape), _const_spec(ng.shape),
                  _const_spec(tri.shape), _const_spec(trit.shape)],
        out_specs=pl.BlockSpec((1, cl, SSD_WIDTH), lambda bi, j: (bi, j, 0)),
        out_shape=jax.ShapeDtypeStruct((b, s, SSD_WIDTH), BF16),
        scratch_shapes=[pltpu.VMEM((cl + 2 * SUBLANES, SSD_XBC), F32),
                        pltpu.VMEM((SSD_HEADS, SSD_STATE, SSD_HEAD_DIM), F32)],
        compiler_params=_cparams("parallel", "arbitrary"),
        name="ssd_mixer",
    )(z3, cw, cb, dtb, a, dsk, ng, tri, trit)


def _out_mlp_kernel(x_ref, a_ref, s_ref, m_ref, wo_ref, g_ref, wu_ref, wd_ref, o_ref, *, fc):
    wo = wo_ref
    mix = jnp.dot(a_ref[...], wo[0:ATTN_WIDTH, :], preferred_element_type=F32)
    mix = mix + jnp.dot(s_ref[...], wo[ATTN_WIDTH:ATTN_WIDTH + S5_WIDTH, :], preferred_element_type=F32)
    mix = mix + jnp.dot(m_ref[...], wo[ATTN_WIDTH + S5_WIDTH:, :], preferred_element_type=F32)
    x1 = x_ref[...] + mix
    ms = jnp.mean(x1 * x1, axis=-1, keepdims=True)
    h = (x1 * lax.rsqrt(ms + EPS) * g_ref[...]).astype(BF16)
    acc = x1
    for c in range(wu_ref.shape[1] // fc):
        up = jnp.maximum(jnp.dot(h, wu_ref[:, c * fc:(c + 1) * fc], preferred_element_type=F32), 0.0)
        acc = acc + jnp.dot((up * up).astype(BF16), wd_ref[c * fc:(c + 1) * fc, :],
                            preferred_element_type=F32)
    o_ref[...] = acc


def _out_mlp(x2d, attn, s5o, ssdo, wo, g, wu, wd, tm, fc):
    n, d = x2d.shape

    def row_spec(width):
        return pl.BlockSpec((tm, width), lambda i: (i, 0))

    def resident(shape):
        return pl.BlockSpec(shape, lambda i: (0, 0), pipeline_mode=pl.Buffered(1))

    return pl.pallas_call(
        functools.partial(_out_mlp_kernel, fc=fc),
        grid=(n // tm,),
        in_specs=[row_spec(d), row_spec(ATTN_WIDTH), row_spec(S5_WIDTH), row_spec(SSD_WIDTH),
                  resident(wo.shape), resident(g.shape), resident(wu.shape), resident(wd.shape)],
        out_specs=row_spec(d),
        out_shape=jax.ShapeDtypeStruct((n, d), F32),
        compiler_params=_cparams("parallel"),
        name="out_proj_mlp",
    )(x2d, attn, s5o, ssdo, wo, g, wu, wd)


def _pad_w_in(w_in_l):
    d = w_in_l.shape[0]
    sizes = (ATTN_WIDTH, KV_WIDTH, KV_WIDTH, IDX_HEADS * IDX_DIM, IDX_DIM, IDX_HEADS,
             S5_WIDTH, SSD_WIDTH, SSD_XBC, SSD_HEADS)
    offs = np.concatenate([[0], np.cumsum(sizes)])
    dst = (OFF_Q, OFF_K, OFF_V, OFF_QI, OFF_KW, OFF_KW + IDX_DIM,
           OFF_S5, OFF_SSD, OFF_SSD + SSD_WIDTH, OFF_SSD + SSD_WIDTH + SSD_XBC)
    out = jnp.zeros((d, N_IN_PAD), F32)
    for i, n in enumerate(sizes):
        out = out.at[:, dst[i]:dst[i] + n].set(w_in_l[:, offs[i]:offs[i] + n])
    return out.astype(BF16)


def _rope_tables(s):
    pos = jnp.arange(s, dtype=F32)
    inv_freq = ROPE_THETA ** (-jnp.arange(0, ROPE_DIM, 2, dtype=F32) / ROPE_DIM)
    ang = pos[:, None] * inv_freq[None, :]
    cos, sin = jnp.cos(ang), jnp.sin(ang)
    zeros = jnp.zeros((s, ROPE_HALF), F32)
    rest = HEAD_DIM - ROPE_DIM
    c = jnp.concatenate([cos, cos, jnp.ones((s, rest), F32)], axis=-1)
    sa = jnp.concatenate([-sin, zeros, jnp.zeros((s, rest), F32)], axis=-1)
    sb = jnp.concatenate([zeros, sin, jnp.zeros((s, rest), F32)], axis=-1)
    return c, sa, sb


def _block_diag_mean(width, seg, active=None):
    idx = np.arange(width)
    m = (idx[:, None] // seg == idx[None, :] // seg).astype(np.float32) / seg
    if active is not None:
        m = m * (idx[:, None] < active) * (idx[None, :] < active)
    return jnp.asarray(m, BF16)


def _pad_lanes(v, fill=0.0):
    v = v.reshape(1, -1).astype(F32)
    return jnp.pad(v, ((0, 0), (0, LANES - v.shape[1])), constant_values=fill)


def _s5_params(lam_re, lam_im, log_step, b_re, b_im, c_re, c_im):
    step = jnp.exp(log_step.astype(F32))[:, None]
    lr, li = lam_re.astype(F32), lam_im.astype(F32)
    mag = jnp.exp(lr * step)
    ab_re = mag * jnp.cos(li * step)
    ab_im = mag * jnp.sin(li * step)
    den = lr * lr + li * li
    cr = ((ab_re - 1.0) * lr + ab_im * li) / den
    ci = (ab_im * lr - (ab_re - 1.0) * li) / den
    bb_re = cr[..., None] * b_re - ci[..., None] * b_im
    bb_im = cr[..., None] * b_im + ci[..., None] * b_re
    eye = jnp.eye(S5_GROUPS, dtype=F32)

    def in_bd(bb):
        return jnp.einsum('gph,gk->ghkp', bb, eye).reshape(S5_WIDTH, S5_LANES)

    def out_bd(cc):
        return jnp.einsum('ghp,gk->gpkh', cc, eye).reshape(S5_LANES, S5_WIDTH)

    wb = jnp.concatenate([in_bd(bb_re), in_bd(bb_im)], axis=1).astype(BF16)
    wc = jnp.concatenate([out_bd(c_re.astype(F32)), -out_bd(c_im.astype(F32))], axis=0).astype(BF16)
    return wb, ab_re.reshape(1, S5_LANES), ab_im.reshape(1, S5_LANES), wc


def _tile(n, pref):
    return pref if n % pref == 0 else n


def kernel(x, norm_mix_g, w_in, attn_q_norm_g, attn_k_norm_g, idx_k_norm_g, s5_lambda_re, s5_lambda_im, s5_log_step, s5_b_re, s5_b_im, s5_c_re, s5_c_im, s5_d, s5_glu_w, s5_glu_b, ssd_conv_w, ssd_conv_b, ssd_dt_bias, ssd_a_log, ssd_d, ssd_norm_g, w_out, norm_mlp_g, w_up, w_down):
    b, s, d = x.shape
    n = b * s
    depth = w_in.shape[0]
    assert b == SUBLANES, "the S5 scan keeps one sequence per sublane"
    tm = _tile(n, 512)
    tt = _tile(s, 512)
    tq = _tile(s, 128)
    ck = _tile(s, 512)
    cl = _tile(s, 128)
    ts5 = _tile(s, 64)

    tables = _rope_tables(s)
    bdq = _block_diag_mean(ATTN_WIDTH, HEAD_DIM)
    bdk = _block_diag_mean(KV_WIDTH, HEAD_DIM)
    bdi = _block_diag_mean(LANES, IDX_DIM, active=IDX_DIM)
    tri = jnp.asarray(np.tril(np.ones((cl, cl), np.float32)))
    trit = jnp.asarray(np.triu(np.ones((cl, cl), np.float32)))

    x2d = x.reshape(n, d)
    for l in range(depth):
        z = _norm_matmul(x2d, norm_mix_g[l].reshape(1, d), _pad_w_in(w_in[l]), tm)
        z3 = z.reshape(b, s, N_IN_PAD)

        qg = jnp.tile(attn_q_norm_g[l].astype(F32), ATTN_HEADS).reshape(1, ATTN_WIDTH)
        kg = jnp.tile(attn_k_norm_g[l].astype(F32), ATTN_KV_HEADS).reshape(1, KV_WIDTH)
        kig = _pad_lanes(idx_k_norm_g[l])
        qh, kh, vh, qih, kir, wi = _prep(z3, tables, qg, kg, kig, bdq, bdk, bdi, tt)
        attn = _dsa(qh, kh, vh, qih, kir, wi, tq, ck)

        wb, ar, ai, wc = _s5_params(s5_lambda_re[l], s5_lambda_im[l], s5_log_step[l],
                                    s5_b_re[l], s5_b_im[l], s5_c_re[l], s5_c_im[l])
        u_t = z3[:, :, OFF_S5:OFF_S5 + S5_WIDTH].transpose(1, 0, 2).reshape(n, S5_WIDTH)
        s5_t = _s5(u_t, wb, ar, ai, wc, s5_d[l].reshape(1, S5_WIDTH).astype(F32),
                   s5_glu_w[l].astype(BF16), s5_glu_b[l].reshape(1, S5_WIDTH).astype(F32), b, ts5)
        s5o = s5_t.reshape(s, b, S5_WIDTH).transpose(1, 0, 2).reshape(n, S5_WIDTH)

        ssdo = _ssd(z3, ssd_conv_w[l].astype(F32), ssd_conv_b[l].reshape(1, SSD_XBC).astype(F32),
                    _pad_lanes(ssd_dt_bias[l]), _pad_lanes(-jnp.exp(ssd_a_log[l].astype(F32))),
                    _pad_lanes(ssd_d[l]), ssd_norm_g[l].reshape(1, SSD_WIDTH).astype(F32),
                    tri, trit, cl)

        x2d = _out_mlp(x2d, attn.reshape(n, ATTN_WIDTH), s5o, ssdo.reshape(n, SSD_WIDTH),
                       w_out[l].astype(BF16), norm_mlp_g[l].reshape(1, d).astype(F32),
                       w_up[l].astype(BF16), w_down[l].astype(BF16), tm, 1024)
    return x2d.reshape(b, s, d)
```

```python
import functools
import math

import jax
import jax.numpy as jnp
import numpy as np
from jax import lax
from jax.experimental import pallas as pl
from jax.experimental.pallas import tpu as pltpu

F32 = jnp.float32
BF16 = jnp.bfloat16

HEAD_DIM = 64
ATTN_HEADS = 8
ATTN_KV_HEADS = 2
ATTN_REP = ATTN_HEADS // ATTN_KV_HEADS
ATTN_WIDTH = ATTN_HEADS * HEAD_DIM
KV_WIDTH = ATTN_KV_HEADS * HEAD_DIM
ATTN_SCALE = HEAD_DIM ** -0.5
ROPE_DIM = HEAD_DIM // 4
ROPE_HALF = ROPE_DIM // 2
ROPE_THETA = 500000.0
IDX_HEADS = 8
IDX_DIM = HEAD_DIM
IDX_W_SCALE = (IDX_HEADS ** -0.5) * (IDX_DIM ** -0.5)
TOPK_MAX = 256
S5_WIDTH = 256
S5_GROUP_CH = 16
S5_GROUPS = S5_WIDTH // S5_GROUP_CH
S5_STATE = 64
S5_LANES = S5_GROUPS * S5_STATE
SSD_WIDTH = 256
SSD_HEAD_DIM = 64
SSD_HEADS = SSD_WIDTH // SSD_HEAD_DIM
SSD_NGROUPS = 2
SSD_STATE = 64
SSD_CONV = 4
SSD_XBC = SSD_WIDTH + 2 * SSD_NGROUPS * SSD_STATE
EPS = 1e-6
LANES = 128
SUBLANES = 8

ATTN_SLAB = 1536
OFF_Q, OFF_K, OFF_V, OFF_QI, OFF_KW = 0, 512, 640, 768, 1280
OFF_S5 = ATTN_SLAB
SSD_SLAB = 896
OFF_SSD = OFF_S5 + S5_WIDTH
N_IN_PAD = OFF_SSD + SSD_SLAB

INT32_MIN = -2 ** 31
INT32_MAX = 2 ** 31 - 1
ZERO_BAND_LO = -2 ** 23
LOG2E = math.log2(math.e)
NEG_BIG = -1e30
COUNT_CHAINS = 4
VMEM_LIMIT = 56 * 1024 * 1024


def _cparams(*sem):
    return pltpu.CompilerParams(dimension_semantics=sem, vmem_limit_bytes=VMEM_LIMIT)


def _const_spec(shape):
    nd = len(shape)
    return pl.BlockSpec(shape, lambda *_: (0,) * nd)


def _norm_matmul_kernel(x_ref, g_ref, w_ref, o_ref):
    x = x_ref[...]
    ms = jnp.mean(x * x, axis=-1, keepdims=True)
    h = (x * lax.rsqrt(ms + EPS) * g_ref[...]).astype(BF16)
    o_ref[...] = jnp.dot(h, w_ref[...], preferred_element_type=F32)


def _norm_matmul(x2d, g, w, tm):
    n, d = x2d.shape
    npad = w.shape[1]
    return pl.pallas_call(
        _norm_matmul_kernel,
        grid=(n // tm,),
        in_specs=[pl.BlockSpec((tm, d), lambda i: (i, 0)),
                  _const_spec((1, d)),
                  _const_spec((d, npad))],
        out_specs=pl.BlockSpec((tm, npad), lambda i: (i, 0)),
        out_shape=jax.ShapeDtypeStruct((n, npad), F32),
        compiler_params=_cparams("parallel"),
        name="norm_in_proj",
    )(x2d, g, w)


def _segment_mean(sq, bd_ref):
    hi = sq.astype(BF16)
    lo = (sq - hi.astype(F32)).astype(BF16)
    bd = bd_ref[...]
    return (jnp.dot(hi, bd, preferred_element_type=F32)
            + jnp.dot(lo, bd, preferred_element_type=F32))


def _rope(x, c, sa, sb):
    w = x.shape[-1]
    reps = w // HEAD_DIM
    c, sa, sb = (jnp.tile(t, (1, reps)) if reps > 1 else t for t in (c, sa, sb))
    return x * c + pltpu.roll(x, w - ROPE_HALF, 1) * sa + pltpu.roll(x, ROPE_HALF, 1) * sb


def _prep_kernel(z_ref, c_ref, sa_ref, sb_ref, qg_ref, kg_ref, kig_ref,
                 bdq_ref, bdk_ref, bdi_ref,
                 qt_out, k_out, vt_out, qit_out, ki_out, wt_out, *, tq):
    z = z_ref[0]
    tt = z.shape[0]
    c, sa, sb = c_ref[...], sa_ref[...], sb_ref[...]
    q = z[:, OFF_Q:OFF_Q + ATTN_WIDTH]
    qn = q * lax.rsqrt(_segment_mean(q * q, bdq_ref) + EPS) * qg_ref[...]
    qn = _rope(qn, c, sa, sb) * (ATTN_SCALE * LOG2E)
    k = z[:, OFF_K:OFF_K + KV_WIDTH]
    kn = k * lax.rsqrt(_segment_mean(k * k, bdk_ref) + EPS) * kg_ref[...]
    kn = _rope(kn, c, sa, sb)
    for h in range(ATTN_KV_HEADS):
        k_out[0, h] = kn[:, h * HEAD_DIM:(h + 1) * HEAD_DIM].astype(BF16)
    v = z[:, OFF_V:OFF_V + KV_WIDTH]
    qi = _rope(z[:, OFF_QI:OFF_QI + IDX_HEADS * IDX_DIM], c, sa, sb)
    kw = z[:, OFF_KW:OFF_KW + LANES]
    kin = kw * lax.rsqrt(_segment_mean(kw * kw, bdi_ref) + EPS) * kig_ref[...]
    kin = _rope(kin, c, sa, sb)
    ki_out[0] = kin[:, 0:IDX_DIM].astype(BF16)

    for i in range(tt // tq):
        rows = slice(i * tq, (i + 1) * tq)
        qt = qn[rows].T
        qit = qi[rows].T
        for h in range(ATTN_HEADS):
            g, r = divmod(h, ATTN_REP)
            qt_out[0, i, g, :, r * tq:(r + 1) * tq] = qt[h * HEAD_DIM:(h + 1) * HEAD_DIM].astype(BF16)
            qit_out[0, i, h] = qit[h * IDX_DIM:(h + 1) * IDX_DIM].astype(BF16)
        wt_out[0, i] = kw[rows].T[IDX_DIM:IDX_DIM + IDX_HEADS] * IDX_W_SCALE
    for i in range(tt // LANES):
        vt = v[i * LANES:(i + 1) * LANES].T
        for g in range(ATTN_KV_HEADS):
            vt_out[0, g, i] = vt[g * HEAD_DIM:(g + 1) * HEAD_DIM].astype(BF16)


def _prep(z3, tables, qg, kg, kig, bdq, bdk, bdi, tt, tq):
    b, s, _ = z3.shape
    nq, nsub, nv = s // tq, tt // tq, tt // LANES
    tab_spec = pl.BlockSpec((tt, HEAD_DIM), lambda bi, j: (j, 0))
    return pl.pallas_call(
        functools.partial(_prep_kernel, tq=tq),
        grid=(b, s // tt),
        in_specs=[pl.BlockSpec((1, tt, ATTN_SLAB), lambda bi, j: (bi, j, 0)),
                  tab_spec, tab_spec, tab_spec,
                  _const_spec(qg.shape), _const_spec(kg.shape), _const_spec(kig.shape),
                  _const_spec(bdq.shape), _const_spec(bdk.shape), _const_spec(bdi.shape)],
        out_specs=[pl.BlockSpec((1, nsub, ATTN_KV_HEADS, HEAD_DIM, ATTN_REP * tq),
                                lambda bi, j: (bi, j, 0, 0, 0)),
                   pl.BlockSpec((1, ATTN_KV_HEADS, tt, HEAD_DIM), lambda bi, j: (bi, 0, j, 0)),
                   pl.BlockSpec((1, ATTN_KV_HEADS, nv, HEAD_DIM, LANES), lambda bi, j: (bi, 0, j, 0, 0)),
                   pl.BlockSpec((1, nsub, IDX_HEADS, IDX_DIM, tq), lambda bi, j: (bi, j, 0, 0, 0)),
                   pl.BlockSpec((1, tt, IDX_DIM), lambda bi, j: (bi, j, 0)),
                   pl.BlockSpec((1, nsub, IDX_HEADS, tq), lambda bi, j: (bi, j, 0, 0))],
        out_shape=[jax.ShapeDtypeStruct((b, nq, ATTN_KV_HEADS, HEAD_DIM, ATTN_REP * tq), BF16),
                   jax.ShapeDtypeStruct((b, ATTN_KV_HEADS, s, HEAD_DIM), BF16),
                   jax.ShapeDtypeStruct((b, ATTN_KV_HEADS, s // LANES, HEAD_DIM, LANES), BF16),
                   jax.ShapeDtypeStruct((b, nq, IDX_HEADS, IDX_DIM, tq), BF16),
                   jax.ShapeDtypeStruct((b, s, IDX_DIM), BF16),
                   jax.ShapeDtypeStruct((b, nq, IDX_HEADS, tq), F32)],
        compiler_params=_cparams("parallel", "parallel"),
        name="attn_prep",
    )(z3, *tables, qg, kg, kig, bdq, bdk, bdi)


def _order_key(x):
    return x ^ (lax.shift_right_arithmetic(x, 31) & 0x7FFFFFFF)


def _f32_to_key(v):
    return _order_key(lax.bitcast_convert_type(v, jnp.int32))


def _key_to_f32(key):
    return lax.bitcast_convert_type(_order_key(key), F32)


def _dsa_kernel(qi_ref, w_ref, q_ref, ki_ref, k_ref, v_ref, o_ref,
                sc_ref, thr_ref, m_ref, acc_ref, *, tq, ck, topk, max_rounds):
    qb = pl.program_id(1)
    n_chunks = lax.div(qb * tq, jnp.int32(ck)) + 1
    nr = ck // SUBLANES
    shape3 = (nr, SUBLANES, tq)
    q_pos = qb * tq + lax.broadcasted_iota(jnp.int32, (1, tq), 1)
    q_pos_b = jnp.broadcast_to(q_pos, (SUBLANES, tq))[None]
    key_id = (lax.broadcasted_iota(jnp.int32, shape3, 0) * SUBLANES
              + lax.broadcasted_iota(jnp.int32, shape3, 1))

    w = w_ref[0, 0]
    wb = [jnp.broadcast_to(w[h:h + 1, :], (SUBLANES, tq))[None] for h in range(IDX_HEADS)]

    def score_chunk(c, carry, diag):
        kmax, kmin = carry
        off = pl.multiple_of(c * ck, ck)
        kic = ki_ref[0, pl.ds(off, ck), :]
        sc = None
        for h in range(IDX_HEADS):
            lg = jnp.dot(kic, qi_ref[0, 0, h], preferred_element_type=F32)
            t = jnp.maximum(lg.reshape(shape3), 0.0) * wb[h]
            sc = t if sc is None else sc + t
        kid = key_id + off
        key = jnp.where(sc == 0.0, -2 - kid, _f32_to_key(sc))
        if diag:
            adm = kid <= q_pos_b
            kmin = jnp.minimum(kmin, jnp.min(jnp.where(adm, key, INT32_MAX), axis=0))
            key = jnp.where(adm, key, INT32_MIN)
        else:
            kmin = jnp.minimum(kmin, jnp.min(key, axis=0))
        kmax = jnp.maximum(kmax, jnp.max(key, axis=0))
        sc_ref[c] = key.reshape(ck, tq)
        return kmax, kmin

    ext = (jnp.full((SUBLANES, tq), INT32_MIN, jnp.int32), jnp.full((SUBLANES, tq), INT32_MAX, jnp.int32))
    ext = lax.fori_loop(0, n_chunks - 1, functools.partial(score_chunk, diag=False), ext)
    kmax, kmin = score_chunk(n_chunks - 1, ext, True)
    kmax = jnp.max(kmax, axis=0, keepdims=True)
    kmin = jnp.min(kmin, axis=0, keepdims=True)

    def count(pred):
        def body(c, acc):
            hits = pred(sc_ref[c].reshape(shape3), c).astype(jnp.int32)
            return acc + jnp.sum(hits.reshape(nr // COUNT_CHAINS, COUNT_CHAINS, SUBLANES, tq), axis=0)
        acc = lax.fori_loop(0, n_chunks, body, jnp.zeros((COUNT_CHAINS, SUBLANES, tq), jnp.int32))
        return jnp.sum(jnp.sum(acc, axis=0), axis=0, keepdims=True)

    def count_ge(kx):
        kx_b = jnp.broadcast_to(kx, (SUBLANES, tq))[None]
        return count(lambda x, c: x >= kx_b)

    n_adm = q_pos + 1
    c_pos = count_ge(jnp.zeros((1, tq), jnp.int32))
    c_nn = count_ge(jnp.full((1, tq), ZERO_BAND_LO, jnp.int32))
    band = jnp.logical_and(c_pos < topk, c_nn > topk)
    pos = c_pos > topk
    all_in = n_adm <= topk
    done0 = jnp.logical_or(all_in, jnp.logical_or(c_pos == topk, c_nn == topk))
    thr0 = jnp.where(all_in, INT32_MIN + 1, jnp.where(c_pos == topk, 0, ZERO_BAND_LO))
    n_cols = (n_chunks * ck).astype(F32)
    lo0 = jnp.where(band, -(n_cols + 2.0), jnp.where(pos, 0.0, _key_to_f32(kmin)))
    hi0 = jnp.where(band, -1.0, jnp.where(pos, _key_to_f32(kmax), -jnp.finfo(F32).tiny))
    clo0 = jnp.where(band, c_nn, jnp.where(pos, c_pos, n_adm)).astype(F32)
    chi0 = jnp.where(band, c_pos, jnp.where(pos, 1, c_nn)).astype(F32)
    log_k = math.log(topk - 0.5)

    def search_step(st, bisect):
        lo, hi, c_lo, c_hi, done, thr = st
        if bisect:
            frac = 0.5
        else:
            lin = (c_lo - (topk - 0.5)) / (c_lo - c_hi)
            lg_lo = jnp.log(c_lo)
            lg = (lg_lo - log_k) / (lg_lo - jnp.log(jnp.maximum(c_hi, 0.5)))
            frac = jnp.clip(jnp.where(band, lin, lg), 0.0, 1.0)
        x = lo + (hi - lo) * frac
        x = jnp.where(band, jnp.floor(x), x)
        kx = jnp.where(band, x.astype(jnp.int32), _f32_to_key(x))
        cnt = count_ge(kx)
        hit = jnp.logical_and(cnt == topk, done == 0)
        thr = jnp.where(hit, kx, thr)
        done = jnp.where(hit, 1, done)
        above = cnt > topk
        cf = cnt.astype(F32)
        return (jnp.where(above, x, lo), jnp.where(above, hi, x),
                jnp.where(above, cf, c_lo), jnp.where(above, c_hi, cf), done, thr)

    def search_cond(st):
        return jnp.logical_and(st[0] < max_rounds, jnp.min(st[1][4]) == 0)

    def search_round(st):
        inner = search_step(search_step(search_step(st[1], False), False), True)
        return st[0] + 1, inner

    st = lax.while_loop(search_cond, search_round,
                        (jnp.int32(0), (lo0, hi0, clo0, chi0, done0.astype(jnp.int32), thr0)))
    thr_ref[...] = st[1][5]

    @pl.when(jnp.min(st[1][4]) == 0)
    def _():
        def bisect_value(_, carry):
            lo, hi, c_lo, c_hi = carry
            mid = lo + lax.shift_right_logical(hi - lo, 1)
            cnt = count_ge(mid)
            ge = cnt >= topk
            return (jnp.where(ge, mid, lo), jnp.where(ge, hi, mid),
                    jnp.where(ge, cnt, c_lo), jnp.where(ge, c_hi, cnt))

        init = (jnp.full((1, tq), INT32_MIN + 1, jnp.int32), jnp.full((1, tq), INT32_MAX, jnp.int32),
                jnp.full((1, tq), topk, jnp.int32), jnp.zeros((1, tq), jnp.int32))
        thr, _, c_ge, c_gt = lax.fori_loop(0, 32, bisect_value, init)
        thr_ref[...] = thr
        need = c_ge > topk
        want = topk - c_gt
        thr_b = jnp.broadcast_to(thr, (SUBLANES, tq))[None]

        def bisect_index(_, carry):
            jlo, jhi = carry
            mid = jlo + lax.shift_right_logical(jhi - jlo, 1)
            mid_b = jnp.broadcast_to(mid, (SUBLANES, tq))[None]
            cnt = count(lambda x, c: jnp.logical_and(x == thr_b, key_id + c * ck <= mid_b))
            ge = cnt >= want
            return jnp.where(ge, jlo, mid), jnp.where(ge, mid, jhi)

        n_idx_steps = int(math.ceil(math.log2(sc_ref.shape[0] * ck + 1)))
        jinit = (jnp.full((1, tq), -1, jnp.int32), (n_chunks * ck - 1) + jnp.zeros((1, tq), jnp.int32))
        _, cut = lax.fori_loop(0, n_idx_steps, bisect_index, jinit)
        cut_b = jnp.broadcast_to(jnp.where(need, cut, INT32_MAX), (SUBLANES, tq))[None]

        def drop(c, carry):
            x = sc_ref[c].reshape(shape3)
            gone = jnp.logical_and(x == thr_b, key_id + c * ck > cut_b)
            sc_ref[c] = jnp.where(gone, INT32_MIN, x).reshape(ck, tq)
            return carry

        lax.fori_loop(0, n_chunks, drop, 0)

    thr_b = jnp.broadcast_to(thr_ref[...], (SUBLANES, tq))[None]
    m_ref[...] = jnp.full(m_ref.shape, NEG_BIG, F32)
    acc_ref[...] = jnp.zeros(acc_ref.shape, F32)
    ones_rows = jnp.ones((acc_ref.shape[1] - HEAD_DIM, ck), BF16)
    vblocks = ck // LANES

    def attend_chunk(c, carry):
        off = pl.multiple_of(c * ck, ck)
        bias = jnp.where(sc_ref[c].reshape(shape3) >= thr_b, 0.0, NEG_BIG).reshape(ck, tq)
        for g in range(ATTN_KV_HEADS):
            kc = k_ref[0, g, pl.ds(off, ck), :]
            vt = jnp.concatenate([v_ref[0, g, c * vblocks + i] for i in range(vblocks)], axis=1)
            vext = jnp.concatenate([vt, ones_rows], axis=0)
            s = jnp.dot(kc, q_ref[0, 0, g], preferred_element_type=F32)
            ps, alphas = [], []
            for r in range(ATTN_REP):
                h = g * ATTN_REP + r
                sh = s[:, r * tq:(r + 1) * tq] + bias
                m_old = m_ref[h]
                m_new = jnp.maximum(m_old, jnp.max(jnp.max(sh.reshape(shape3), axis=0), axis=0, keepdims=True))
                ps.append(jnp.exp2(sh - m_new).astype(BF16))
                alphas.append(jnp.exp2(m_old - m_new))
                m_ref[h] = m_new
            acc_ref[g] = (jnp.concatenate(alphas, axis=1) * acc_ref[g]
                          + jnp.dot(vext, jnp.concatenate(ps, axis=1), preferred_element_type=F32))
        return carry

    lax.fori_loop(0, n_chunks, attend_chunk, 0)

    outs = []
    for g in range(ATTN_KV_HEADS):
        a = acc_ref[g]
        o = a[0:HEAD_DIM] / a[HEAD_DIM:HEAD_DIM + 1]
        outs += [o[:, r * tq:(r + 1) * tq] for r in range(ATTN_REP)]
    o_ref[0] = jnp.concatenate(outs, axis=0).T.astype(o_ref.dtype)


def _dsa(qt, kh, vt, qit, kir, wt, tq, ck):
    b, _, s, _ = kh.shape
    assert ck % tq == 0, "only the last causal chunk may hold inadmissible keys"
    topk = min(TOPK_MAX, s // 4)
    kernel = functools.partial(_dsa_kernel, tq=tq, ck=ck, topk=topk, max_rounds=12)
    return pl.pallas_call(
        kernel,
        grid=(b, s // tq),
        in_specs=[pl.BlockSpec((1, 1, IDX_HEADS, IDX_DIM, tq), lambda bi, j: (bi, j, 0, 0, 0)),
                  pl.BlockSpec((1, 1, IDX_HEADS, tq), lambda bi, j: (bi, j, 0, 0)),
                  pl.BlockSpec((1, 1, ATTN_KV_HEADS, HEAD_DIM, ATTN_REP * tq), lambda bi, j: (bi, j, 0, 0, 0)),
                  pl.BlockSpec((1, s, IDX_DIM), lambda bi, j: (bi, 0, 0)),
                  pl.BlockSpec((1, ATTN_KV_HEADS, s, HEAD_DIM), lambda bi, j: (bi, 0, 0, 0)),
                  pl.BlockSpec((1, ATTN_KV_HEADS, s // LANES, HEAD_DIM, LANES),
                               lambda bi, j: (bi, 0, 0, 0, 0))],
        out_specs=pl.BlockSpec((1, tq, ATTN_WIDTH), lambda bi, j: (bi, j, 0)),
        out_shape=jax.ShapeDtypeStruct((b, s, ATTN_WIDTH), BF16),
        scratch_shapes=[pltpu.VMEM((s // ck, ck, tq), jnp.int32),
                        pltpu.VMEM((1, tq), jnp.int32),
                        pltpu.VMEM((ATTN_HEADS, 1, tq), F32),
                        pltpu.VMEM((ATTN_KV_HEADS, HEAD_DIM + 2 * SUBLANES, ATTN_REP * tq), F32)],
        compiler_params=_cparams("parallel", "arbitrary"),
        name="dsa_mixer",
    )(qit, wt, qt, kir, kh, vt)


def _gelu_tanh(y):
    return 0.5 * y * (1.0 + jnp.tanh(math.sqrt(2.0 / math.pi) * (y + 0.044715 * (y * y * y))))


def _s5_kernel(u_ref, wb_ref, ar_ref, ai_ref, wc_ref, d_ref, gw_ref, gb_ref, o_ref,
               x_ref, st_ref, *, tt, nb):
    @pl.when(pl.program_id(0) == 0)
    def _():
        st_ref[...] = jnp.zeros(st_ref.shape, F32)

    u = u_ref[...]
    x_ref[...] = jnp.dot(u.astype(BF16), wb_ref[...], preferred_element_type=F32)
    ar = jnp.broadcast_to(ar_ref[...], (nb, S5_LANES))
    ai = jnp.broadcast_to(ai_ref[...], (nb, S5_LANES))

    def step(t, carry):
        xr, xi = carry
        r0 = pl.multiple_of(t * nb, nb)
        nxr = ar * xr - ai * xi + x_ref[pl.ds(r0, nb), 0:S5_LANES]
        nxi = ar * xi + ai * xr + x_ref[pl.ds(r0, nb), S5_LANES:2 * S5_LANES]
        x_ref[pl.ds(r0, nb), 0:S5_LANES] = nxr
        x_ref[pl.ds(r0, nb), S5_LANES:2 * S5_LANES] = nxi
        return nxr, nxi

    xr, xi = lax.fori_loop(0, tt, step, (st_ref[0:nb, :], st_ref[nb:2 * nb, :]), unroll=4)
    st_ref[0:nb, :] = xr
    st_ref[nb:2 * nb, :] = xi

    y = jnp.dot(x_ref[...].astype(BF16), wc_ref[...], preferred_element_type=F32)
    y = _gelu_tanh(y + d_ref[...] * u)
    gate = jnp.dot(y.astype(BF16), gw_ref[...], preferred_element_type=F32) + gb_ref[...]
    o_ref[...] = (y * jax.nn.sigmoid(gate)).astype(o_ref.dtype)


def _s5(u_t, wb, ar, ai, wc, d, gw, gb, nb, tt):
    n = u_t.shape[0]
    rows = nb * tt
    kernel = functools.partial(_s5_kernel, tt=tt, nb=nb)
    return pl.pallas_call(
        kernel,
        grid=(n // rows,),
        in_specs=[pl.BlockSpec((rows, S5_WIDTH), lambda i: (i, 0)),
                  _const_spec(wb.shape), _const_spec(ar.shape), _const_spec(ai.shape),
                  _const_spec(wc.shape), _const_spec(d.shape), _const_spec(gw.shape),
                  _const_spec(gb.shape)],
        out_specs=pl.BlockSpec((rows, S5_WIDTH), lambda i: (i, 0)),
        out_shape=jax.ShapeDtypeStruct((n, S5_WIDTH), BF16),
        scratch_shapes=[pltpu.VMEM((rows, 2 * S5_LANES), F32),
                        pltpu.VMEM((2 * nb, S5_LANES), F32)],
        compiler_params=_cparams("arbitrary"),
        name="s5_mixer",
    )(u_t, wb, ar, ai, wc, d, gw, gb)


_NT = (((1,), (1,)), ((), ()))
_TN = (((0,), (0,)), ((), ()))
_HI = lax.Precision.HIGHEST


def _ssd_kernel(z_ref, cw_ref, cb_ref, dtb_ref, a_ref, dsk_ref, ng_ref, tri_ref, trit_ref,
                o_ref, ext_ref, st_ref, *, cl):
    @pl.when(pl.program_id(1) == 0)
    def _():
        st_ref[...] = jnp.zeros(st_ref.shape, F32)
        ext_ref[0:SUBLANES, :] = jnp.zeros((SUBLANES, SSD_XBC), F32)

    tile = z_ref[0]
    zg = tile[:, 0:SSD_WIDTH]
    ext_ref[SUBLANES:SUBLANES + cl, :] = tile[:, SSD_WIDTH:SSD_WIDTH + SSD_XBC]
    conv = cb_ref[...]
    for k in range(SSD_CONV):
        start = SUBLANES - (SSD_CONV - 1) + k
        conv = conv + cw_ref[k:k + 1, :] * ext_ref[start:start + cl, :]
    ext_ref[0:SUBLANES, :] = ext_ref[cl:cl + SUBLANES, :]
    xa = conv * jax.nn.sigmoid(conv)
    xs = xa[:, 0:SSD_WIDTH]
    bm = xa[:, SSD_WIDTH:SSD_WIDTH + SSD_NGROUPS * SSD_STATE]
    cm = xa[:, SSD_WIDTH + SSD_NGROUPS * SSD_STATE:SSD_XBC]

    dtx = tile[:, SSD_WIDTH + SSD_XBC:SSD_SLAB] + dtb_ref[...]
    dt = jnp.maximum(dtx, 0.0) + jnp.log1p(jnp.exp(-jnp.abs(dtx)))
    a_dt = dt * a_ref[...]
    acs = jnp.dot(tri_ref[...], a_dt, precision=_HI, preferred_element_type=F32)
    acs_row = jnp.dot(a_dt.T[0:SUBLANES], trit_ref[...], precision=_HI, preferred_element_type=F32)
    causal = (lax.broadcasted_iota(jnp.int32, (cl, cl), 0)
              >= lax.broadcasted_iota(jnp.int32, (cl, cl), 1))

    cb_scores = []
    for g in range(SSD_NGROUPS):
        bg = bm[:, g * SSD_STATE:(g + 1) * SSD_STATE].astype(BF16)
        cg = cm[:, g * SSD_STATE:(g + 1) * SSD_STATE].astype(BF16)
        cb_scores.append(lax.dot_general(cg, bg, _NT, preferred_element_type=F32))

    rep = SSD_HEADS // SSD_NGROUPS
    ys = []
    for h in range(SSD_HEADS):
        g = h // rep
        acol = acs[:, h:h + 1]
        arow = acs_row[h:h + 1, :]
        alast = acs[cl - 1:cl, h:h + 1]
        decay = jnp.exp(jnp.where(causal, acol - arow, -jnp.inf))
        xs_h = xs[:, h * SSD_HEAD_DIM:(h + 1) * SSD_HEAD_DIM]
        xdt = (xs_h * dt[:, h:h + 1]).astype(BF16)
        y = jnp.dot((cb_scores[g] * decay).astype(BF16), xdt, preferred_element_type=F32)
        bg = bm[:, g * SSD_STATE:(g + 1) * SSD_STATE]
        cg = cm[:, g * SSD_STATE:(g + 1) * SSD_STATE].astype(BF16)
        prev = st_ref[h]
        y = y + jnp.dot(cg, prev.astype(BF16), preferred_element_type=F32) * jnp.exp(acol)
        bdec = (bg * jnp.exp(alast - acol)).astype(BF16)
        st_ref[h] = prev * jnp.exp(alast) + lax.dot_general(bdec, xdt, _TN, preferred_element_type=F32)
        ys.append(y + dsk_ref[:, h:h + 1] * xs_h)

    y = jnp.concatenate(ys, axis=-1) * (zg * jax.nn.sigmoid(zg))
    gw = SSD_WIDTH // SSD_NGROUPS
    outs = []
    for g in range(SSD_NGROUPS):
        yg = y[:, g * gw:(g + 1) * gw]
        outs.append(yg * lax.rsqrt(jnp.mean(yg * yg, axis=-1, keepdims=True) + EPS))
    o_ref[0] = (jnp.concatenate(outs, axis=-1) * ng_ref[...]).astype(o_ref.dtype)


def _ssd(z3, cw, cb, dtb, a, dsk, ng, tri, trit, cl):
    b, s, _ = z3.shape
    kernel = functools.partial(_ssd_kernel, cl=cl)
    return pl.pallas_call(
        kernel,
        grid=(b, s // cl),
        in_specs=[pl.BlockSpec((1, cl, SSD_SLAB), lambda bi, j: (bi, j, OFF_SSD // SSD_SLAB)),
                  _const_spec(cw.shape), _const_spec(cb.shape), _const_spec(dtb.shape),
                  _const_spec(a.shape), _const_spec(dsk.shape), _const_spec(ng.shape),
                  _const_spec(tri.shape), _const_spec(trit.shape)],
        out_specs=pl.BlockSpec((1, cl, SSD_WIDTH), lambda bi, j: (bi, j, 0)),
        out_shape=jax.ShapeDtypeStruct((b, s, SSD_WIDTH), BF16),
        scratch_shapes=[pltpu.VMEM((cl + 2 * SUBLANES, SSD_XBC), F32),
                        pltpu.VMEM((SSD_HEADS, SSD_STATE, SSD_HEAD_DIM), F32)],
        compiler_params=_cparams("parallel", "arbitrary"),
        name="ssd_mixer",
    )(z3, cw, cb, dtb, a, dsk, ng, tri, trit)


def _out_mlp_kernel(x_ref, a_ref, s_ref, m_ref, wo_ref, g_ref, wu_ref, wd_ref, o_ref, *, fc):
    wo = wo_ref
    mix = jnp.dot(a_ref[...], wo[0:ATTN_WIDTH, :], preferred_element_type=F32)
    mix = mix + jnp.dot(s_ref[...], wo[ATTN_WIDTH:ATTN_WIDTH + S5_WIDTH, :], preferred_element_type=F32)
    mix = mix + jnp.dot(m_ref[...], wo[ATTN_WIDTH + S5_WIDTH:, :], preferred_element_type=F32)
    x1 = x_ref[...] + mix
    ms = jnp.mean(x1 * x1, axis=-1, keepdims=True)
    h = (x1 * lax.rsqrt(ms + EPS) * g_ref[...]).astype(BF16)
    acc = x1
    for c in range(wu_ref.shape[1] // fc):
        up = jnp.maximum(jnp.dot(h, wu_ref[:, c * fc:(c + 1) * fc], preferred_element_type=F32), 0.0)
        acc = acc + jnp.dot((up * up).astype(BF16), wd_ref[c * fc:(c + 1) * fc, :],
                            preferred_element_type=F32)
    o_ref[...] = acc


def _out_mlp(x2d, attn, s5o, ssdo, wo, g, wu, wd, tm, fc):
    n, d = x2d.shape

    def row_spec(width):
        return pl.BlockSpec((tm, width), lambda i: (i, 0))

    def resident(shape):
        return pl.BlockSpec(shape, lambda i: (0, 0), pipeline_mode=pl.Buffered(1))

    return pl.pallas_call(
        functools.partial(_out_mlp_kernel, fc=fc),
        grid=(n // tm,),
        in_specs=[row_spec(d), row_spec(ATTN_WIDTH), row_spec(S5_WIDTH), row_spec(SSD_WIDTH),
                  resident(wo.shape), resident(g.shape), resident(wu.shape), resident(wd.shape)],
        out_specs=row_spec(d),
        out_shape=jax.ShapeDtypeStruct((n, d), F32),
        compiler_params=_cparams("parallel"),
        name="out_proj_mlp",
    )(x2d, attn, s5o, ssdo, wo, g, wu, wd)


def _pad_w_in(w_in_l):
    d = w_in_l.shape[0]
    sizes = (ATTN_WIDTH, KV_WIDTH, KV_WIDTH, IDX_HEADS * IDX_DIM, IDX_DIM, IDX_HEADS,
             S5_WIDTH, SSD_WIDTH, SSD_XBC, SSD_HEADS)
    offs = np.concatenate([[0], np.cumsum(sizes)])
    dst = (OFF_Q, OFF_K, OFF_V, OFF_QI, OFF_KW, OFF_KW + IDX_DIM,
           OFF_S5, OFF_SSD, OFF_SSD + SSD_WIDTH, OFF_SSD + SSD_WIDTH + SSD_XBC)
    out = jnp.zeros((d, N_IN_PAD), F32)
    for i, n in enumerate(sizes):
        out = out.at[:, dst[i]:dst[i] + n].set(w_in_l[:, offs[i]:offs[i] + n])
    return out.astype(BF16)


def _rope_tables(s):
    pos = jnp.arange(s, dtype=F32)
    inv_freq = ROPE_THETA ** (-jnp.arange(0, ROPE_DIM, 2, dtype=F32) / ROPE_DIM)
    ang = pos[:, None] * inv_freq[None, :]
    cos, sin = jnp.cos(ang), jnp.sin(ang)
    zeros = jnp.zeros((s, ROPE_HALF), F32)
    rest = HEAD_DIM - ROPE_DIM
    c = jnp.concatenate([cos, cos, jnp.ones((s, rest), F32)], axis=-1)
    sa = jnp.concatenate([-sin, zeros, jnp.zeros((s, rest), F32)], axis=-1)
    sb = jnp.concatenate([zeros, sin, jnp.zeros((s, rest), F32)], axis=-1)
    return c, sa, sb


def _block_diag_mean(width, seg, active=None):
    idx = np.arange(width)
    m = (idx[:, None] // seg == idx[None, :] // seg).astype(np.float32) / seg
    if active is not None:
        m = m * (idx[:, None] < active) * (idx[None, :] < active)
    return jnp.asarray(m, BF16)


def _pad_lanes(v, fill=0.0):
    v = v.reshape(1, -1).astype(F32)
    return jnp.pad(v, ((0, 0), (0, LANES - v.shape[1])), constant_values=fill)


def _s5_params(lam_re, lam_im, log_step, b_re, b_im, c_re, c_im):
    step = jnp.exp(log_step.astype(F32))[:, None]
    lr, li = lam_re.astype(F32), lam_im.astype(F32)
    mag = jnp.exp(lr * step)
    ab_re = mag * jnp.cos(li * step)
    ab_im = mag * jnp.sin(li * step)
    den = lr * lr + li * li
    cr = ((ab_re - 1.0) * lr + ab_im * li) / den
    ci = (ab_im * lr - (ab_re - 1.0) * li) / den
    bb_re = cr[..., None] * b_re - ci[..., None] * b_im
    bb_im = cr[..., None] * b_im + ci[..., None] * b_re
    eye = jnp.eye(S5_GROUPS, dtype=F32)

    def in_bd(bb):
        return jnp.einsum('gph,gk->ghkp', bb, eye).reshape(S5_WIDTH, S5_LANES)

    def out_bd(cc):
        return jnp.einsum('ghp,gk->gpkh', cc, eye).reshape(S5_LANES, S5_WIDTH)

    wb = jnp.concatenate([in_bd(bb_re), in_bd(bb_im)], axis=1).astype(BF16)
    wc = jnp.concatenate([out_bd(c_re.astype(F32)), -out_bd(c_im.astype(F32))], axis=0).astype(BF16)
    return wb, ab_re.reshape(1, S5_LANES), ab_im.reshape(1, S5_LANES), wc


def _tile(n, pref):
    return pref if n % pref == 0 else n


def kernel(x, norm_mix_g, w_in, attn_q_norm_g, attn_k_norm_g, idx_k_norm_g, s5_lambda_re, s5_lambda_im, s5_log_step, s5_b_re, s5_b_im, s5_c_re, s5_c_im, s5_d, s5_glu_w, s5_glu_b, ssd_conv_w, ssd_conv_b, ssd_dt_bias, ssd_a_log, ssd_d, ssd_norm_g, w_out, norm_mlp_g, w_up, w_down):
    b, s, d = x.shape
    n = b * s
    depth = w_in.shape[0]
    assert b == SUBLANES, "the S5 scan keeps one sequence per sublane"
    tm = _tile(n, 512)
    tt = _tile(s, 512)
    tq = _tile(s, 256)
    ck = _tile(s, 512)
    cl = _tile(s, 128)
    ts5 = _tile(s, 64)

    tables = _rope_tables(s)
    bdq = _block_diag_mean(ATTN_WIDTH, HEAD_DIM)
    bdk = _block_diag_mean(KV_WIDTH, HEAD_DIM)
    bdi = _block_diag_mean(LANES, IDX_DIM, active=IDX_DIM)
    tri = jnp.asarray(np.tril(np.ones((cl, cl), np.float32)))
    trit = jnp.asarray(np.triu(np.ones((cl, cl), np.float32)))

    x2d = x.reshape(n, d)
    for l in range(depth):
        z = _norm_matmul(x2d, norm_mix_g[l].reshape(1, d), _pad_w_in(w_in[l]), tm)
        z3 = z.reshape(b, s, N_IN_PAD)

        qg = jnp.tile(attn_q_norm_g[l].astype(F32), ATTN_HEADS).reshape(1, ATTN_WIDTH)
        kg = jnp.tile(attn_k_norm_g[l].astype(F32), ATTN_KV_HEADS).reshape(1, KV_WIDTH)
        kig = _pad_lanes(idx_k_norm_g[l])
        qt, kh, vt, qit, kir, wt = _prep(z3, tables, qg, kg, kig, bdq, bdk, bdi, tt, tq)
        attn = _dsa(qt, kh, vt, qit, kir, wt, tq, ck)

        wb, ar, ai, wc = _s5_params(s5_lambda_re[l], s5_lambda_im[l], s5_log_step[l],
                                    s5_b_re[l], s5_b_im[l], s5_c_re[l], s5_c_im[l])
        u_t = z3[:, :, OFF_S5:OFF_S5 + S5_WIDTH].transpose(1, 0, 2).reshape(n, S5_WIDTH)
        s5_t = _s5(u_t, wb, ar, ai, wc, s5_d[l].reshape(1, S5_WIDTH).astype(F32),
                   s5_glu_w[l].astype(BF16), s5_glu_b[l].reshape(1, S5_WIDTH).astype(F32), b, ts5)
        s5o = s5_t.reshape(s, b, S5_WIDTH).transpose(1, 0, 2).reshape(n, S5_WIDTH)

        ssdo = _ssd(z3, ssd_conv_w[l].astype(F32), ssd_conv_b[l].reshape(1, SSD_XBC).astype(F32),
                    _pad_lanes(ssd_dt_bias[l]), _pad_lanes(-jnp.exp(ssd_a_log[l].astype(F32))),
                    _pad_lanes(ssd_d[l]), ssd_norm_g[l].reshape(1, SSD_WIDTH).astype(F32),
                    tri, trit, cl)

        x2d = _out_mlp(x2d, attn.reshape(n, ATTN_WIDTH), s5o, ssdo.reshape(n, SSD_WIDTH),
                       w_out[l].astype(BF16), norm_mlp_g[l].reshape(1, d).astype(F32),
                       w_up[l].astype(BF16), w_down[l].astype(BF16), tm, 1024)
    return x2d.reshape(b, s, d)
```

```python
import functools
import math

import jax
import jax.numpy as jnp
import numpy as np
from jax import lax
from jax.experimental import pallas as pl
from jax.experimental.pallas import tpu as pltpu

F32 = jnp.float32
BF16 = jnp.bfloat16

HEAD_DIM = 64
ATTN_HEADS = 8
ATTN_KV_HEADS = 2
ATTN_REP = ATTN_HEADS // ATTN_KV_HEADS
ATTN_WIDTH = ATTN_HEADS * HEAD_DIM
KV_WIDTH = ATTN_KV_HEADS * HEAD_DIM
ATTN_SCALE = HEAD_DIM ** -0.5
ROPE_DIM = HEAD_DIM // 4
ROPE_HALF = ROPE_DIM // 2
ROPE_THETA = 500000.0
IDX_HEADS = 8
IDX_DIM = HEAD_DIM
IDX_W_SCALE = (IDX_HEADS ** -0.5) * (IDX_DIM ** -0.5)
TOPK_MAX = 256
S5_WIDTH = 256
S5_GROUP_CH = 16
S5_GROUPS = S5_WIDTH // S5_GROUP_CH
S5_STATE = 64
S5_LANES = S5_GROUPS * S5_STATE
SSD_WIDTH = 256
SSD_HEAD_DIM = 64
SSD_HEADS = SSD_WIDTH // SSD_HEAD_DIM
SSD_NGROUPS = 2
SSD_STATE = 64
SSD_CONV = 4
SSD_XBC = SSD_WIDTH + 2 * SSD_NGROUPS * SSD_STATE
EPS = 1e-6
LANES = 128
SUBLANES = 8

ATTN_SLAB = 1536
OFF_Q, OFF_K, OFF_V, OFF_QI, OFF_KW = 0, 512, 640, 768, 1280
OFF_S5 = ATTN_SLAB
SSD_SLAB = 896
OFF_SSD = OFF_S5 + S5_WIDTH
N_IN_PAD = OFF_SSD + SSD_SLAB

INT32_MIN = -2 ** 31
INT32_MAX = 2 ** 31 - 1
ZERO_BAND_LO = -2 ** 23
LOG2E = math.log2(math.e)
NEG_BIG = -1e30
COUNT_CHAINS = 4
ATTN_SUB_KEYS = 512
HEADS_PER_DOT = 4
VMEM_LIMIT = 56 * 1024 * 1024


def _cparams(*sem):
    return pltpu.CompilerParams(dimension_semantics=sem, vmem_limit_bytes=VMEM_LIMIT)


def _const_spec(shape):
    nd = len(shape)
    return pl.BlockSpec(shape, lambda *_: (0,) * nd)


def _norm_matmul_kernel(x_ref, g_ref, w_ref, o_ref):
    x = x_ref[...]
    ms = jnp.mean(x * x, axis=-1, keepdims=True)
    h = (x * lax.rsqrt(ms + EPS) * g_ref[...]).astype(BF16)
    o_ref[...] = jnp.dot(h, w_ref[...], preferred_element_type=F32)


def _norm_matmul(x2d, g, w, tm):
    n, d = x2d.shape
    npad = w.shape[1]
    return pl.pallas_call(
        _norm_matmul_kernel,
        grid=(n // tm,),
        in_specs=[pl.BlockSpec((tm, d), lambda i: (i, 0)),
                  _const_spec((1, d)),
                  _const_spec((d, npad))],
        out_specs=pl.BlockSpec((tm, npad), lambda i: (i, 0)),
        out_shape=jax.ShapeDtypeStruct((n, npad), F32),
        compiler_params=_cparams("parallel"),
        name="norm_in_proj",
    )(x2d, g, w)


def _segment_mean(sq, bd_ref):
    hi = sq.astype(BF16)
    lo = (sq - hi.astype(F32)).astype(BF16)
    bd = bd_ref[...]
    return (jnp.dot(hi, bd, preferred_element_type=F32)
            + jnp.dot(lo, bd, preferred_element_type=F32))


def _rope(x, c, sa, sb):
    w = x.shape[-1]
    reps = w // HEAD_DIM
    c, sa, sb = (jnp.tile(t, (1, reps)) if reps > 1 else t for t in (c, sa, sb))
    return x * c + pltpu.roll(x, w - ROPE_HALF, 1) * sa + pltpu.roll(x, ROPE_HALF, 1) * sb


def _prep_kernel(z_ref, c_ref, sa_ref, sb_ref, qg_ref, kg_ref, kig_ref,
                 bdq_ref, bdk_ref, bdi_ref,
                 qt_out, k_out, vt_out, qit_out, ki_out, wt_out, *, tq):
    z = z_ref[0]
    tt = z.shape[0]
    c, sa, sb = c_ref[...], sa_ref[...], sb_ref[...]
    q = z[:, OFF_Q:OFF_Q + ATTN_WIDTH]
    qn = q * lax.rsqrt(_segment_mean(q * q, bdq_ref) + EPS) * qg_ref[...]
    qn = _rope(qn, c, sa, sb) * (ATTN_SCALE * LOG2E)
    k = z[:, OFF_K:OFF_K + KV_WIDTH]
    kn = k * lax.rsqrt(_segment_mean(k * k, bdk_ref) + EPS) * kg_ref[...]
    kn = _rope(kn, c, sa, sb)
    for h in range(ATTN_KV_HEADS):
        k_out[0, h] = kn[:, h * HEAD_DIM:(h + 1) * HEAD_DIM].astype(BF16)
    v = z[:, OFF_V:OFF_V + KV_WIDTH]
    qi = _rope(z[:, OFF_QI:OFF_QI + IDX_HEADS * IDX_DIM], c, sa, sb)
    kw = z[:, OFF_KW:OFF_KW + LANES]
    kin = kw * lax.rsqrt(_segment_mean(kw * kw, bdi_ref) + EPS) * kig_ref[...]
    kin = _rope(kin, c, sa, sb)
    ki_out[0] = kin[:, 0:IDX_DIM].astype(BF16)

    for i in range(tt // tq):
        rows = slice(i * tq, (i + 1) * tq)
        qt = qn[rows].T
        qit = qi[rows].T
        for h in range(ATTN_HEADS):
            g, r = divmod(h, ATTN_REP)
            qt_out[0, i, g, :, r * tq:(r + 1) * tq] = qt[h * HEAD_DIM:(h + 1) * HEAD_DIM].astype(BF16)
            qit_out[0, i, h] = qit[h * IDX_DIM:(h + 1) * IDX_DIM].astype(BF16)
        wt_out[0, i] = kw[rows].T[IDX_DIM:IDX_DIM + IDX_HEADS] * IDX_W_SCALE
    for i in range(tt // LANES):
        vt = v[i * LANES:(i + 1) * LANES].T
        for g in range(ATTN_KV_HEADS):
            vt_out[0, g, i] = vt[g * HEAD_DIM:(g + 1) * HEAD_DIM].astype(BF16)


def _prep(z3, tables, qg, kg, kig, bdq, bdk, bdi, tt, tq):
    b, s, _ = z3.shape
    nq, nsub, nv = s // tq, tt // tq, tt // LANES
    tab_spec = pl.BlockSpec((tt, HEAD_DIM), lambda bi, j: (j, 0))
    return pl.pallas_call(
        functools.partial(_prep_kernel, tq=tq),
        grid=(b, s // tt),
        in_specs=[pl.BlockSpec((1, tt, ATTN_SLAB), lambda bi, j: (bi, j, 0)),
                  tab_spec, tab_spec, tab_spec,
                  _const_spec(qg.shape), _const_spec(kg.shape), _const_spec(kig.shape),
                  _const_spec(bdq.shape), _const_spec(bdk.shape), _const_spec(bdi.shape)],
        out_specs=[pl.BlockSpec((1, nsub, ATTN_KV_HEADS, HEAD_DIM, ATTN_REP * tq),
                                lambda bi, j: (bi, j, 0, 0, 0)),
                   pl.BlockSpec((1, ATTN_KV_HEADS, tt, HEAD_DIM), lambda bi, j: (bi, 0, j, 0)),
                   pl.BlockSpec((1, ATTN_KV_HEADS, nv, HEAD_DIM, LANES), lambda bi, j: (bi, 0, j, 0, 0)),
                   pl.BlockSpec((1, nsub, IDX_HEADS, IDX_DIM, tq), lambda bi, j: (bi, j, 0, 0, 0)),
                   pl.BlockSpec((1, tt, IDX_DIM), lambda bi, j: (bi, j, 0)),
                   pl.BlockSpec((1, nsub, IDX_HEADS, tq), lambda bi, j: (bi, j, 0, 0))],
        out_shape=[jax.ShapeDtypeStruct((b, nq, ATTN_KV_HEADS, HEAD_DIM, ATTN_REP * tq), BF16),
                   jax.ShapeDtypeStruct((b, ATTN_KV_HEADS, s, HEAD_DIM), BF16),
                   jax.ShapeDtypeStruct((b, ATTN_KV_HEADS, s // LANES, HEAD_DIM, LANES), BF16),
                   jax.ShapeDtypeStruct((b, nq, IDX_HEADS, IDX_DIM, tq), BF16),
                   jax.ShapeDtypeStruct((b, s, IDX_DIM), BF16),
                   jax.ShapeDtypeStruct((b, nq, IDX_HEADS, tq), F32)],
        compiler_params=_cparams("parallel", "parallel"),
        name="attn_prep",
    )(z3, *tables, qg, kg, kig, bdq, bdk, bdi)


def _order_key(x):
    return x ^ (lax.shift_right_arithmetic(x, 31) & 0x7FFFFFFF)


def _f32_to_key(v):
    return _order_key(lax.bitcast_convert_type(v, jnp.int32))


def _key_to_f32(key):
    return lax.bitcast_convert_type(_order_key(key), F32)


def _dsa_kernel(qi_ref, w_ref, q_ref, ki_ref, k_ref, v_ref, o_ref,
                sc_ref, thr_ref, m_ref, acc_ref, *, tq, ck, topk, max_rounds):
    qb = pl.program_id(1)
    n_chunks = lax.div(qb * tq, jnp.int32(ck)) + 1
    nr = ck // SUBLANES
    shape3 = (nr, SUBLANES, tq)
    q_pos = qb * tq + lax.broadcasted_iota(jnp.int32, (1, tq), 1)
    q_pos_b = jnp.broadcast_to(q_pos, (SUBLANES, tq))[None]
    key_id = (lax.broadcasted_iota(jnp.int32, shape3, 0) * SUBLANES
              + lax.broadcasted_iota(jnp.int32, shape3, 1))

    w = w_ref[0, 0]
    wb = [jnp.broadcast_to(w[h:h + 1, :], (SUBLANES, tq))[None] for h in range(IDX_HEADS)]

    def score_chunk(c, carry, diag):
        kmax, kmin = carry
        off = pl.multiple_of(c * ck, ck)
        kic = ki_ref[0, pl.ds(off, ck), :]
        sc = None
        for h in range(IDX_HEADS):
            lg = jnp.dot(kic, qi_ref[0, 0, h], preferred_element_type=F32)
            t = jnp.maximum(lg.reshape(shape3), 0.0) * wb[h]
            sc = t if sc is None else sc + t
        kid = key_id + off
        key = jnp.where(sc == 0.0, -2 - kid, _f32_to_key(sc))
        if diag:
            adm = kid <= q_pos_b
            kmin = jnp.minimum(kmin, jnp.min(jnp.where(adm, key, INT32_MAX), axis=0))
            key = jnp.where(adm, key, INT32_MIN)
        else:
            kmin = jnp.minimum(kmin, jnp.min(key, axis=0))
        kmax = jnp.maximum(kmax, jnp.max(key, axis=0))
        sc_ref[c] = key.reshape(ck, tq)
        return kmax, kmin

    ext = (jnp.full((SUBLANES, tq), INT32_MIN, jnp.int32), jnp.full((SUBLANES, tq), INT32_MAX, jnp.int32))
    ext = lax.fori_loop(0, n_chunks - 1, functools.partial(score_chunk, diag=False), ext)
    kmax, kmin = score_chunk(n_chunks - 1, ext, True)
    kmax = jnp.max(kmax, axis=0, keepdims=True)
    kmin = jnp.min(kmin, axis=0, keepdims=True)

    def count(pred):
        def body(c, acc):
            hits = pred(sc_ref[c].reshape(shape3), c).astype(jnp.int32)
            return acc + jnp.sum(hits.reshape(nr // COUNT_CHAINS, COUNT_CHAINS, SUBLANES, tq), axis=0)
        acc = lax.fori_loop(0, n_chunks, body, jnp.zeros((COUNT_CHAINS, SUBLANES, tq), jnp.int32))
        return jnp.sum(jnp.sum(acc, axis=0), axis=0, keepdims=True)

    def count_ge(kx):
        kx_b = jnp.broadcast_to(kx, (SUBLANES, tq))[None]
        return count(lambda x, c: x >= kx_b)

    n_adm = q_pos + 1
    c_pos = count_ge(jnp.zeros((1, tq), jnp.int32))
    c_nn = count_ge(jnp.full((1, tq), ZERO_BAND_LO, jnp.int32))
    band = jnp.logical_and(c_pos < topk, c_nn > topk)
    pos = c_pos > topk
    all_in = n_adm <= topk
    done0 = jnp.logical_or(all_in, jnp.logical_or(c_pos == topk, c_nn == topk))
    thr0 = jnp.where(all_in, INT32_MIN + 1, jnp.where(c_pos == topk, 0, ZERO_BAND_LO))
    n_cols = (n_chunks * ck).astype(F32)
    lo0 = jnp.where(band, -(n_cols + 2.0), jnp.where(pos, 0.0, _key_to_f32(kmin)))
    hi0 = jnp.where(band, -1.0, jnp.where(pos, _key_to_f32(kmax), -jnp.finfo(F32).tiny))
    clo0 = jnp.where(band, c_nn, jnp.where(pos, c_pos, n_adm)).astype(F32)
    chi0 = jnp.where(band, c_pos, jnp.where(pos, 1, c_nn)).astype(F32)
    log_k = math.log(topk - 0.5)

    def search_step(st, bisect):
        lo, hi, c_lo, c_hi, done, thr = st
        if bisect:
            frac = 0.5
        else:
            lin = (c_lo - (topk - 0.5)) / (c_lo - c_hi)
            lg_lo = jnp.log(c_lo)
            lg = (lg_lo - log_k) / (lg_lo - jnp.log(jnp.maximum(c_hi, 0.5)))
            frac = jnp.clip(jnp.where(band, lin, lg), 0.0, 1.0)
        x = lo + (hi - lo) * frac
        x = jnp.where(band, jnp.floor(x), x)
        kx = jnp.where(band, x.astype(jnp.int32), _f32_to_key(x))
        cnt = count_ge(kx)
        hit = jnp.logical_and(cnt == topk, done == 0)
        thr = jnp.where(hit, kx, thr)
        done = jnp.where(hit, 1, done)
        above = cnt > topk
        cf = cnt.astype(F32)
        return (jnp.where(above, x, lo), jnp.where(above, hi, x),
                jnp.where(above, cf, c_lo), jnp.where(above, c_hi, cf), done, thr)

    def search_cond(st):
        return jnp.logical_and(st[0] < max_rounds, jnp.min(st[1][4]) == 0)

    def search_round(st):
        inner = search_step(search_step(search_step(st[1], False), False), True)
        return st[0] + 1, inner

    st = lax.while_loop(search_cond, search_round,
                        (jnp.int32(0), (lo0, hi0, clo0, chi0, done0.astype(jnp.int32), thr0)))
    thr_ref[...] = st[1][5]

    @pl.when(jnp.min(st[1][4]) == 0)
    def _():
        def bisect_value(_, carry):
            lo, hi, c_lo, c_hi = carry
            mid = lo + lax.shift_right_logical(hi - lo, 1)
            cnt = count_ge(mid)
            ge = cnt >= topk
            return (jnp.where(ge, mid, lo), jnp.where(ge, hi, mid),
                    jnp.where(ge, cnt, c_lo), jnp.where(ge, c_hi, cnt))

        init = (jnp.full((1, tq), INT32_MIN + 1, jnp.int32), jnp.full((1, tq), INT32_MAX, jnp.int32),
                jnp.full((1, tq), topk, jnp.int32), jnp.zeros((1, tq), jnp.int32))
        thr, _, c_ge, c_gt = lax.fori_loop(0, 32, bisect_value, init)
        thr_ref[...] = thr
        need = c_ge > topk
        want = topk - c_gt
        thr_b = jnp.broadcast_to(thr, (SUBLANES, tq))[None]

        def bisect_index(_, carry):
            jlo, jhi = carry
            mid = jlo + lax.shift_right_logical(jhi - jlo, 1)
            mid_b = jnp.broadcast_to(mid, (SUBLANES, tq))[None]
            cnt = count(lambda x, c: jnp.logical_and(x == thr_b, key_id + c * ck <= mid_b))
            ge = cnt >= want
            return jnp.where(ge, jlo, mid), jnp.where(ge, mid, jhi)

        n_idx_steps = int(math.ceil(math.log2(sc_ref.shape[0] * ck + 1)))
        jinit = (jnp.full((1, tq), -1, jnp.int32), (n_chunks * ck - 1) + jnp.zeros((1, tq), jnp.int32))
        _, cut = lax.fori_loop(0, n_idx_steps, bisect_index, jinit)
        cut_b = jnp.broadcast_to(jnp.where(need, cut, INT32_MAX), (SUBLANES, tq))[None]

        def drop(c, carry):
            x = sc_ref[c].reshape(shape3)
            gone = jnp.logical_and(x == thr_b, key_id + c * ck > cut_b)
            sc_ref[c] = jnp.where(gone, INT32_MIN, x).reshape(ck, tq)
            return carry

        lax.fori_loop(0, n_chunks, drop, 0)

    thr_b = jnp.broadcast_to(thr_ref[...], (SUBLANES, tq))[None]
    m_ref[...] = jnp.full(m_ref.shape, NEG_BIG, F32)
    acc_ref[...] = jnp.zeros(acc_ref.shape, F32)
    ak = min(ATTN_SUB_KEYS, ck)
    ones_rows = jnp.ones((acc_ref.shape[1] - HEAD_DIM, ak), BF16)
    vblocks = ck // LANES

    def attend_chunk(c, carry):
        off = pl.multiple_of(c * ck, ck)
        for sub in range(ck // ak):
            r0 = sub * ak
            sel = sc_ref[c, r0:r0 + ak, :].reshape(ak // SUBLANES, SUBLANES, tq) >= thr_b
            bias = jnp.where(sel, 0.0, NEG_BIG).reshape(ak, tq)
            for g in range(ATTN_KV_HEADS):
                kc = k_ref[0, g, pl.ds(off + r0, ak), :]
                vt = jnp.concatenate([v_ref[0, g, c * vblocks + sub * (ak // LANES) + i]
                                      for i in range(ak // LANES)], axis=1)
                vext = jnp.concatenate([vt, ones_rows], axis=0)
                for hp in range(ATTN_REP // HEADS_PER_DOT):
                    l0, l1 = hp * HEADS_PER_DOT * tq, (hp + 1) * HEADS_PER_DOT * tq
                    s = jnp.dot(kc, q_ref[0, 0, g, :, l0:l1], preferred_element_type=F32)
                    ps, alphas = [], []
                    for r in range(HEADS_PER_DOT):
                        h = g * ATTN_REP + hp * HEADS_PER_DOT + r
                        sh = s[:, r * tq:(r + 1) * tq] + bias
                        m_old = m_ref[h]
                        m_new = jnp.maximum(m_old, jnp.max(
                            jnp.max(sh.reshape(ak // SUBLANES, SUBLANES, tq), axis=0), axis=0, keepdims=True))
                        ps.append(jnp.exp2(sh - m_new).astype(BF16))
                        alphas.append(jnp.exp2(m_old - m_new))
                        m_ref[h] = m_new
                    acc_ref[g, :, l0:l1] = (jnp.concatenate(alphas, axis=1) * acc_ref[g, :, l0:l1]
                                            + jnp.dot(vext, jnp.concatenate(ps, axis=1),
                                                      preferred_element_type=F32))
        return carry

    lax.fori_loop(0, n_chunks, attend_chunk, 0)

    outs = []
    for g in range(ATTN_KV_HEADS):
        a = acc_ref[g]
        o = a[0:HEAD_DIM] / a[HEAD_DIM:HEAD_DIM + 1]
        outs += [o[:, r * tq:(r + 1) * tq] for r in range(ATTN_REP)]
    o_ref[0] = jnp.concatenate(outs, axis=0).T.astype(o_ref.dtype)


def _dsa(qt, kh, vt, qit, kir, wt, tq, ck):
    b, _, s, _ = kh.shape
    assert ck % tq == 0, "only the last causal chunk may hold inadmissible keys"
    topk = min(TOPK_MAX, s // 4)
    kernel = functools.partial(_dsa_kernel, tq=tq, ck=ck, topk=topk, max_rounds=12)
    return pl.pallas_call(
        kernel,
        grid=(b, s // tq),
        in_specs=[pl.BlockSpec((1, 1, IDX_HEADS, IDX_DIM, tq), lambda bi, j: (bi, j, 0, 0, 0)),
                  pl.BlockSpec((1, 1, IDX_HEADS, tq), lambda bi, j: (bi, j, 0, 0)),
                  pl.BlockSpec((1, 1, ATTN_KV_HEADS, HEAD_DIM, ATTN_REP * tq), lambda bi, j: (bi, j, 0, 0, 0)),
                  pl.BlockSpec((1, s, IDX_DIM), lambda bi, j: (bi, 0, 0)),
                  pl.BlockSpec((1, ATTN_KV_HEADS, s, HEAD_DIM), lambda bi, j: (bi, 0, 0, 0)),
                  pl.BlockSpec((1, ATTN_KV_HEADS, s // LANES, HEAD_DIM, LANES),
                               lambda bi, j: (bi, 0, 0, 0, 0))],
        out_specs=pl.BlockSpec((1, tq, ATTN_WIDTH), lambda bi, j: (bi, j, 0)),
        out_shape=jax.ShapeDtypeStruct((b, s, ATTN_WIDTH), BF16),
        scratch_shapes=[pltpu.VMEM((s // ck, ck, tq), jnp.int32),
                        pltpu.VMEM((1, tq), jnp.int32),
                        pltpu.VMEM((ATTN_HEADS, 1, tq), F32),
                        pltpu.VMEM((ATTN_KV_HEADS, HEAD_DIM + 2 * SUBLANES, ATTN_REP * tq), F32)],
        compiler_params=_cparams("parallel", "arbitrary"),
        name="dsa_mixer",
    )(qit, wt, qt, kir, kh, vt)


def _gelu_tanh(y):
    return 0.5 * y * (1.0 + jnp.tanh(math.sqrt(2.0 / math.pi) * (y + 0.044715 * (y * y * y))))


def _s5_kernel(u_ref, wb_ref, ar_ref, ai_ref, wc_ref, d_ref, gw_ref, gb_ref, o_ref,
               x_ref, st_ref, *, tt, nb):
    @pl.when(pl.program_id(0) == 0)
    def _():
        st_ref[...] = jnp.zeros(st_ref.shape, F32)

    u = u_ref[...]
    x_ref[...] = jnp.dot(u.astype(BF16), wb_ref[...], preferred_element_type=F32)
    ar = jnp.broadcast_to(ar_ref[...], (nb, S5_LANES))
    ai = jnp.broadcast_to(ai_ref[...], (nb, S5_LANES))

    def step(t, carry):
        xr, xi = carry
        r0 = pl.multiple_of(t * nb, nb)
        nxr = ar * xr - ai * xi + x_ref[pl.ds(r0, nb), 0:S5_LANES]
        nxi = ar * xi + ai * xr + x_ref[pl.ds(r0, nb), S5_LANES:2 * S5_LANES]
        x_ref[pl.ds(r0, nb), 0:S5_LANES] = nxr
        x_ref[pl.ds(r0, nb), S5_LANES:2 * S5_LANES] = nxi
        return nxr, nxi

    xr, xi = lax.fori_loop(0, tt, step, (st_ref[0:nb, :], st_ref[nb:2 * nb, :]), unroll=4)
    st_ref[0:nb, :] = xr
    st_ref[nb:2 * nb, :] = xi

    y = jnp.dot(x_ref[...].astype(BF16), wc_ref[...], preferred_element_type=F32)
    y = _gelu_tanh(y + d_ref[...] * u)
    gate = jnp.dot(y.astype(BF16), gw_ref[...], preferred_element_type=F32) + gb_ref[...]
    o_ref[...] = (y * jax.nn.sigmoid(gate)).astype(o_ref.dtype)


def _s5(u_t, wb, ar, ai, wc, d, gw, gb, nb, tt):
    n = u_t.shape[0]
    rows = nb * tt
    kernel = functools.partial(_s5_kernel, tt=tt, nb=nb)
    return pl.pallas_call(
        kernel,
        grid=(n // rows,),
        in_specs=[pl.BlockSpec((rows, S5_WIDTH), lambda i: (i, 0)),
                  _const_spec(wb.shape), _const_spec(ar.shape), _const_spec(ai.shape),
                  _const_spec(wc.shape), _const_spec(d.shape), _const_spec(gw.shape),
                  _const_spec(gb.shape)],
        out_specs=pl.BlockSpec((rows, S5_WIDTH), lambda i: (i, 0)),
        out_shape=jax.ShapeDtypeStruct((n, S5_WIDTH), BF16),
        scratch_shapes=[pltpu.VMEM((rows, 2 * S5_LANES), F32),
                        pltpu.VMEM((2 * nb, S5_LANES), F32)],
        compiler_params=_cparams("arbitrary"),
        name="s5_mixer",
    )(u_t, wb, ar, ai, wc, d, gw, gb)


_NT = (((1,), (1,)), ((), ()))
_TN = (((0,), (0,)), ((), ()))
_HI = lax.Precision.HIGHEST


def _ssd_kernel(z_ref, cw_ref, cb_ref, dtb_ref, a_ref, dsk_ref, ng_ref, tri_ref, trit_ref,
                o_ref, ext_ref, st_ref, *, cl):
    @pl.when(pl.program_id(1) == 0)
    def _():
        st_ref[...] = jnp.zeros(st_ref.shape, F32)
        ext_ref[0:SUBLANES, :] = jnp.zeros((SUBLANES, SSD_XBC), F32)

    tile = z_ref[0]
    zg = tile[:, 0:SSD_WIDTH]
    ext_ref[SUBLANES:SUBLANES + cl, :] = tile[:, SSD_WIDTH:SSD_WIDTH + SSD_XBC]
    conv = cb_ref[...]
    for k in range(SSD_CONV):
        start = SUBLANES - (SSD_CONV - 1) + k
        conv = conv + cw_ref[k:k + 1, :] * ext_ref[start:start + cl, :]
    ext_ref[0:SUBLANES, :] = ext_ref[cl:cl + SUBLANES, :]
    xa = conv * jax.nn.sigmoid(conv)
    xs = xa[:, 0:SSD_WIDTH]
    bm = xa[:, SSD_WIDTH:SSD_WIDTH + SSD_NGROUPS * SSD_STATE]
    cm = xa[:, SSD_WIDTH + SSD_NGROUPS * SSD_STATE:SSD_XBC]

    dtx = tile[:, SSD_WIDTH + SSD_XBC:SSD_SLAB] + dtb_ref[...]
    dt = jnp.maximum(dtx, 0.0) + jnp.log1p(jnp.exp(-jnp.abs(dtx)))
    a_dt = dt * a_ref[...]
    acs = jnp.dot(tri_ref[...], a_dt, precision=_HI, preferred_element_type=F32)
    acs_row = jnp.dot(a_dt.T[0:SUBLANES], trit_ref[...], precision=_HI, preferred_element_type=F32)
    causal = (lax.broadcasted_iota(jnp.int32, (cl, cl), 0)
              >= lax.broadcasted_iota(jnp.int32, (cl, cl), 1))

    cb_scores = []
    for g in range(SSD_NGROUPS):
        bg = bm[:, g * SSD_STATE:(g + 1) * SSD_STATE].astype(BF16)
        cg = cm[:, g * SSD_STATE:(g + 1) * SSD_STATE].astype(BF16)
        cb_scores.append(lax.dot_general(cg, bg, _NT, preferred_element_type=F32))

    rep = SSD_HEADS // SSD_NGROUPS
    ys = []
    for h in range(SSD_HEADS):
        g = h // rep
        acol = acs[:, h:h + 1]
        arow = acs_row[h:h + 1, :]
        alast = acs[cl - 1:cl, h:h + 1]
        decay = jnp.exp(jnp.where(causal, acol - arow, -jnp.inf))
        xs_h = xs[:, h * SSD_HEAD_DIM:(h + 1) * SSD_HEAD_DIM]
        xdt = (xs_h * dt[:, h:h + 1]).astype(BF16)
        y = jnp.dot((cb_scores[g] * decay).astype(BF16), xdt, preferred_element_type=F32)
        bg = bm[:, g * SSD_STATE:(g + 1) * SSD_STATE]
        cg = cm[:, g * SSD_STATE:(g + 1) * SSD_STATE].astype(BF16)
        prev = st_ref[h]
        y = y + jnp.dot(cg, prev.astype(BF16), preferred_element_type=F32) * jnp.exp(acol)
        bdec = (bg * jnp.exp(alast - acol)).astype(BF16)
        st_ref[h] = prev * jnp.exp(alast) + lax.dot_general(bdec, xdt, _TN, preferred_element_type=F32)
        ys.append(y + dsk_ref[:, h:h + 1] * xs_h)

    y = jnp.concatenate(ys, axis=-1) * (zg * jax.nn.sigmoid(zg))
    gw = SSD_WIDTH // SSD_NGROUPS
    outs = []
    for g in range(SSD_NGROUPS):
        yg = y[:, g * gw:(g + 1) * gw]
        outs.append(yg * lax.rsqrt(jnp.mean(yg * yg, axis=-1, keepdims=True) + EPS))
    o_ref[0] = (jnp.concatenate(outs, axis=-1) * ng_ref[...]).astype(o_ref.dtype)


def _ssd(z3, cw, cb, dtb, a, dsk, ng, tri, trit, cl):
    b, s, _ = z3.shape
    kernel = functools.partial(_ssd_kernel, cl=cl)
    return pl.pallas_call(
        kernel,
        grid=(b, s // cl),
        in_specs=[pl.BlockSpec((1, cl, SSD_SLAB), lambda bi, j: (bi, j, OFF_SSD // SSD_SLAB)),
                  _const_spec(cw.shape), _const_spec(cb.shape), _const_spec(dtb.shape),
                  _const_spec(a.shape), _const_spec(dsk.shape), _const_spec(ng.shape),
                  _const_spec(tri.shape), _const_spec(trit.shape)],
        out_specs=pl.BlockSpec((1, cl, SSD_WIDTH), lambda bi, j: (bi, j, 0)),
        out_shape=jax.ShapeDtypeStruct((b, s, SSD_WIDTH), BF16),
        scratch_shapes=[pltpu.VMEM((cl + 2 * SUBLANES, SSD_XBC), F32),
                        pltpu.VMEM((SSD_HEADS, SSD_STATE, SSD_HEAD_DIM), F32)],
        compiler_params=_cparams("parallel", "arbitrary"),
        name="ssd_mixer",
    )(z3, cw, cb, dtb, a, dsk, ng, tri, trit)


def _out_mlp_kernel(x_ref, a_ref, s_ref, m_ref, wo_ref, g_ref, wu_ref, wd_ref, o_ref, *, fc):
    wo = wo_ref
    mix = jnp.dot(a_ref[...], wo[0:ATTN_WIDTH, :], preferred_element_type=F32)
    mix = mix + jnp.dot(s_ref[...], wo[ATTN_WIDTH:ATTN_WIDTH + S5_WIDTH, :], preferred_element_type=F32)
    mix = mix + jnp.dot(m_ref[...], wo[ATTN_WIDTH + S5_WIDTH:, :], preferred_element_type=F32)
    x1 = x_ref[...] + mix
    ms = jnp.mean(x1 * x1, axis=-1, keepdims=True)
    h = (x1 * lax.rsqrt(ms + EPS) * g_ref[...]).astype(BF16)
    acc = x1
    for c in range(wu_ref.shape[1] // fc):
        up = jnp.maximum(jnp.dot(h, wu_ref[:, c * fc:(c + 1) * fc], preferred_element_type=F32), 0.0)
        acc = acc + jnp.dot((up * up).astype(BF16), wd_ref[c * fc:(c + 1) * fc, :],
                            preferred_element_type=F32)
    o_ref[...] = acc


def _out_mlp(x2d, attn, s5o, ssdo, wo, g, wu, wd, tm, fc):
    n, d = x2d.shape

    def row_spec(width):
        return pl.BlockSpec((tm, width), lambda i: (i, 0))

    def resident(shape):
        return pl.BlockSpec(shape, lambda i: (0, 0), pipeline_mode=pl.Buffered(1))

    return pl.pallas_call(
        functools.partial(_out_mlp_kernel, fc=fc),
        grid=(n // tm,),
        in_specs=[row_spec(d), row_spec(ATTN_WIDTH), row_spec(S5_WIDTH), row_spec(SSD_WIDTH),
                  resident(wo.shape), resident(g.shape), resident(wu.shape), resident(wd.shape)],
        out_specs=row_spec(d),
        out_shape=jax.ShapeDtypeStruct((n, d), F32),
        compiler_params=_cparams("parallel"),
        name="out_proj_mlp",
    )(x2d, attn, s5o, ssdo, wo, g, wu, wd)


def _pad_w_in(w_in_l):
    d = w_in_l.shape[0]
    sizes = (ATTN_WIDTH, KV_WIDTH, KV_WIDTH, IDX_HEADS * IDX_DIM, IDX_DIM, IDX_HEADS,
             S5_WIDTH, SSD_WIDTH, SSD_XBC, SSD_HEADS)
    offs = np.concatenate([[0], np.cumsum(sizes)])
    dst = (OFF_Q, OFF_K, OFF_V, OFF_QI, OFF_KW, OFF_KW + IDX_DIM,
           OFF_S5, OFF_SSD, OFF_SSD + SSD_WIDTH, OFF_SSD + SSD_WIDTH + SSD_XBC)
    out = jnp.zeros((d, N_IN_PAD), F32)
    for i, n in enumerate(sizes):
        out = out.at[:, dst[i]:dst[i] + n].set(w_in_l[:, offs[i]:offs[i] + n])
    return out.astype(BF16)


def _rope_tables(s):
    pos = jnp.arange(s, dtype=F32)
    inv_freq = ROPE_THETA ** (-jnp.arange(0, ROPE_DIM, 2, dtype=F32) / ROPE_DIM)
    ang = pos[:, None] * inv_freq[None, :]
    cos, sin = jnp.cos(ang), jnp.sin(ang)
    zeros = jnp.zeros((s, ROPE_HALF), F32)
    rest = HEAD_DIM - ROPE_DIM
    c = jnp.concatenate([cos, cos, jnp.ones((s, rest), F32)], axis=-1)
    sa = jnp.concatenate([-sin, zeros, jnp.zeros((s, rest), F32)], axis=-1)
    sb = jnp.concatenate([zeros, sin, jnp.zeros((s, rest), F32)], axis=-1)
    return c, sa, sb


def _block_diag_mean(width, seg, active=None):
    idx = np.arange(width)
    m = (idx[:, None] // seg == idx[None, :] // seg).astype(np.float32) / seg
    if active is not None:
        m = m * (idx[:, None] < active) * (idx[None, :] < active)
    return jnp.asarray(m, BF16)


def _pad_lanes(v, fill=0.0):
    v = v.reshape(1, -1).astype(F32)
    return jnp.pad(v, ((0, 0), (0, LANES - v.shape[1])), constant_values=fill)


def _s5_params(lam_re, lam_im, log_step, b_re, b_im, c_re, c_im):
    step = jnp.exp(log_step.astype(F32))[:, None]
    lr, li = lam_re.astype(F32), lam_im.astype(F32)
    mag = jnp.exp(lr * step)
    ab_re = mag * jnp.cos(li * step)
    ab_im = mag * jnp.sin(li * step)
    den = lr * lr + li * li
    cr = ((ab_re - 1.0) * lr + ab_im * li) / den
    ci = (ab_im * lr - (ab_re - 1.0) * li) / den
    bb_re = cr[..., None] * b_re - ci[..., None] * b_im
    bb_im = cr[..., None] * b_im + ci[..., None] * b_re
    eye = jnp.eye(S5_GROUPS, dtype=F32)

    def in_bd(bb):
        return jnp.einsum('gph,gk->ghkp', bb, eye).reshape(S5_WIDTH, S5_LANES)

    def out_bd(cc):
        return jnp.einsum('ghp,gk->gpkh', cc, eye).reshape(S5_LANES, S5_WIDTH)

    wb = jnp.concatenate([in_bd(bb_re), in_bd(bb_im)], axis=1).astype(BF16)
    wc = jnp.concatenate([out_bd(c_re.astype(F32)), -out_bd(c_im.astype(F32))], axis=0).astype(BF16)
    return wb, ab_re.reshape(1, S5_LANES), ab_im.reshape(1, S5_LANES), wc


def _tile(n, pref):
    return pref if n % pref == 0 else n


def kernel(x, norm_mix_g, w_in, attn_q_norm_g, attn_k_norm_g, idx_k_norm_g, s5_lambda_re, s5_lambda_im, s5_log_step, s5_b_re, s5_b_im, s5_c_re, s5_c_im, s5_d, s5_glu_w, s5_glu_b, ssd_conv_w, ssd_conv_b, ssd_dt_bias, ssd_a_log, ssd_d, ssd_norm_g, w_out, norm_mlp_g, w_up, w_down):
    b, s, d = x.shape
    n = b * s
    depth = w_in.shape[0]
    assert b == SUBLANES, "the S5 scan keeps one sequence per sublane"
    tm = _tile(n, 512)
    tt = _tile(s, 512)
    tq = _tile(s, 512)
    ck = _tile(s, 512)
    cl = _tile(s, 128)
    ts5 = _tile(s, 64)

    tables = _rope_tables(s)
    bdq = _block_diag_mean(ATTN_WIDTH, HEAD_DIM)
    bdk = _block_diag_mean(KV_WIDTH, HEAD_DIM)
    bdi = _block_diag_mean(LANES, IDX_DIM, active=IDX_DIM)
    tri = jnp.asarray(np.tril(np.ones((cl, cl), np.float32)))
    trit = jnp.asarray(np.triu(np.ones((cl, cl), np.float32)))

    x2d = x.reshape(n, d)
    for l in range(depth):
        z = _norm_matmul(x2d, norm_mix_g[l].reshape(1, d), _pad_w_in(w_in[l]), tm)
        z3 = z.reshape(b, s, N_IN_PAD)

        qg = jnp.tile(attn_q_norm_g[l].astype(F32), ATTN_HEADS).reshape(1, ATTN_WIDTH)
        kg = jnp.tile(attn_k_norm_g[l].astype(F32), ATTN_KV_HEADS).reshape(1, KV_WIDTH)
        kig = _pad_lanes(idx_k_norm_g[l])
        qt, kh, vt, qit, kir, wt = _prep(z3, tables, qg, kg, kig, bdq, bdk, bdi, tt, tq)
        attn = _dsa(qt, kh, vt, qit, kir, wt, tq, ck)

        wb, ar, ai, wc = _s5_params(s5_lambda_re[l], s5_lambda_im[l], s5_log_step[l],
                                    s5_b_re[l], s5_b_im[l], s5_c_re[l], s5_c_im[l])
        u_t = z3[:, :, OFF_S5:OFF_S5 + S5_WIDTH].transpose(1, 0, 2).reshape(n, S5_WIDTH)
        s5_t = _s5(u_t, wb, ar, ai, wc, s5_d[l].reshape(1, S5_WIDTH).astype(F32),
                   s5_glu_w[l].astype(BF16), s5_glu_b[l].reshape(1, S5_WIDTH).astype(F32), b, ts5)
        s5o = s5_t.reshape(s, b, S5_WIDTH).transpose(1, 0, 2).reshape(n, S5_WIDTH)

        ssdo = _ssd(z3, ssd_conv_w[l].astype(F32), ssd_conv_b[l].reshape(1, SSD_XBC).astype(F32),
                    _pad_lanes(ssd_dt_bias[l]), _pad_lanes(-jnp.exp(ssd_a_log[l].astype(F32))),
                    _pad_lanes(ssd_d[l]), ssd_norm_g[l].reshape(1, SSD_WIDTH).astype(F32),
                    tri, trit, cl)

        x2d = _out_mlp(x2d, attn.reshape(n, ATTN_WIDTH), s5o, ssdo.reshape(n, SSD_WIDTH),
                       w_out[l].astype(BF16), norm_mlp_g[l].reshape(1, d).astype(F32),
                       w_up[l].astype(BF16), w_down[l].astype(BF16), tm, 1024)
    return x2d.reshape(b, s, d)
```

```python
import functools
import math

import jax
import jax.numpy as jnp
import numpy as np
from jax import lax
from jax.experimental import pallas as pl
from jax.experimental.pallas import tpu as pltpu

F32 = jnp.float32
BF16 = jnp.bfloat16

HEAD_DIM = 64
ATTN_HEADS = 8
ATTN_KV_HEADS = 2
ATTN_REP = ATTN_HEADS // ATTN_KV_HEADS
ATTN_WIDTH = ATTN_HEADS * HEAD_DIM
KV_WIDTH = ATTN_KV_HEADS * HEAD_DIM
ATTN_SCALE = HEAD_DIM ** -0.5
ROPE_DIM = HEAD_DIM // 4
ROPE_HALF = ROPE_DIM // 2
ROPE_THETA = 500000.0
IDX_HEADS = 8
IDX_DIM = HEAD_DIM
IDX_W_SCALE = (IDX_HEADS ** -0.5) * (IDX_DIM ** -0.5)
TOPK_MAX = 256
S5_WIDTH = 256
S5_GROUP_CH = 16
S5_GROUPS = S5_WIDTH // S5_GROUP_CH
S5_STATE = 64
S5_LANES = S5_GROUPS * S5_STATE
SSD_WIDTH = 256
SSD_HEAD_DIM = 64
SSD_HEADS = SSD_WIDTH // SSD_HEAD_DIM
SSD_NGROUPS = 2
SSD_STATE = 64
SSD_CONV = 4
SSD_XBC = SSD_WIDTH + 2 * SSD_NGROUPS * SSD_STATE
EPS = 1e-6
LANES = 128
SUBLANES = 8

ATTN_SLAB = 1536
OFF_Q, OFF_K, OFF_V, OFF_QI, OFF_KW = 0, 512, 640, 768, 1280
OFF_S5 = ATTN_SLAB
SSD_SLAB = 896
OFF_SSD = OFF_S5 + S5_WIDTH
N_IN_PAD = OFF_SSD + SSD_SLAB

INT32_MIN = -2 ** 31
INT32_MAX = 2 ** 31 - 1
ZERO_BAND_LO = -2 ** 23
LOG2E = math.log2(math.e)
NEG_BIG = -1e30
COUNT_CHAINS = 4
ATTN_SUB_KEYS = 512
HEADS_PER_DOT = 4
VMEM_LIMIT = 56 * 1024 * 1024


def _cparams(*sem):
    return pltpu.CompilerParams(dimension_semantics=sem, vmem_limit_bytes=VMEM_LIMIT)


def _const_spec(shape):
    nd = len(shape)
    return pl.BlockSpec(shape, lambda *_: (0,) * nd)


def _norm_matmul_kernel(x_ref, g_ref, w_ref, o_ref):
    x = x_ref[...]
    ms = jnp.mean(x * x, axis=-1, keepdims=True)
    h = (x * lax.rsqrt(ms + EPS) * g_ref[...]).astype(BF16)
    o_ref[...] = jnp.dot(h, w_ref[...], preferred_element_type=F32)


def _norm_matmul(x2d, g, w, tm):
    n, d = x2d.shape
    npad = w.shape[1]
    return pl.pallas_call(
        _norm_matmul_kernel,
        grid=(n // tm,),
        in_specs=[pl.BlockSpec((tm, d), lambda i: (i, 0)),
                  _const_spec((1, d)),
                  _const_spec((d, npad))],
        out_specs=pl.BlockSpec((tm, npad), lambda i: (i, 0)),
        out_shape=jax.ShapeDtypeStruct((n, npad), F32),
        compiler_params=_cparams("parallel"),
        name="norm_in_proj",
    )(x2d, g, w)


def _segment_mean(sq, bd_ref):
    hi = sq.astype(BF16)
    lo = (sq - hi.astype(F32)).astype(BF16)
    bd = bd_ref[...]
    return (jnp.dot(hi, bd, preferred_element_type=F32)
            + jnp.dot(lo, bd, preferred_element_type=F32))


def _rope(x, c, sa, sb):
    w = x.shape[-1]
    reps = w // HEAD_DIM
    c, sa, sb = (jnp.tile(t, (1, reps)) if reps > 1 else t for t in (c, sa, sb))
    return x * c + pltpu.roll(x, w - ROPE_HALF, 1) * sa + pltpu.roll(x, ROPE_HALF, 1) * sb


def _prep_kernel(z_ref, c_ref, sa_ref, sb_ref, qg_ref, kg_ref, kig_ref,
                 bdq_ref, bdk_ref, bdi_ref,
                 qt_out, k_out, vt_out, qit_out, ki_out, wt_out, *, tq):
    z = z_ref[0]
    tt = z.shape[0]
    c, sa, sb = c_ref[...], sa_ref[...], sb_ref[...]
    q = z[:, OFF_Q:OFF_Q + ATTN_WIDTH]
    qn = q * lax.rsqrt(_segment_mean(q * q, bdq_ref) + EPS) * qg_ref[...]
    qn = _rope(qn, c, sa, sb) * (ATTN_SCALE * LOG2E)
    k = z[:, OFF_K:OFF_K + KV_WIDTH]
    kn = k * lax.rsqrt(_segment_mean(k * k, bdk_ref) + EPS) * kg_ref[...]
    kn = _rope(kn, c, sa, sb)
    for h in range(ATTN_KV_HEADS):
        k_out[0, h] = kn[:, h * HEAD_DIM:(h + 1) * HEAD_DIM].astype(BF16)
    v = z[:, OFF_V:OFF_V + KV_WIDTH]
    qi = _rope(z[:, OFF_QI:OFF_QI + IDX_HEADS * IDX_DIM], c, sa, sb)
    kw = z[:, OFF_KW:OFF_KW + LANES]
    kin = kw * lax.rsqrt(_segment_mean(kw * kw, bdi_ref) + EPS) * kig_ref[...]
    kin = _rope(kin, c, sa, sb)
    ki_out[0] = kin[:, 0:IDX_DIM].astype(BF16)

    for i in range(tt // tq):
        rows = slice(i * tq, (i + 1) * tq)
        qt = qn[rows].T
        qit = qi[rows].T
        for h in range(ATTN_HEADS):
            g, r = divmod(h, ATTN_REP)
            qt_out[0, i, g, :, r * tq:(r + 1) * tq] = qt[h * HEAD_DIM:(h + 1) * HEAD_DIM].astype(BF16)
            qit_out[0, i, h] = qit[h * IDX_DIM:(h + 1) * IDX_DIM].astype(BF16)
        wt_out[0, i] = kw[rows].T[IDX_DIM:IDX_DIM + IDX_HEADS] * IDX_W_SCALE
    for i in range(tt // LANES):
        vt = v[i * LANES:(i + 1) * LANES].T
        for g in range(ATTN_KV_HEADS):
            vt_out[0, g, i] = vt[g * HEAD_DIM:(g + 1) * HEAD_DIM].astype(BF16)


def _prep(z3, tables, qg, kg, kig, bdq, bdk, bdi, tt, tq):
    b, s, _ = z3.shape
    nq, nsub, nv = s // tq, tt // tq, tt // LANES
    tab_spec = pl.BlockSpec((tt, HEAD_DIM), lambda bi, j: (j, 0))
    return pl.pallas_call(
        functools.partial(_prep_kernel, tq=tq),
        grid=(b, s // tt),
        in_specs=[pl.BlockSpec((1, tt, ATTN_SLAB), lambda bi, j: (bi, j, 0)),
                  tab_spec, tab_spec, tab_spec,
                  _const_spec(qg.shape), _const_spec(kg.shape), _const_spec(kig.shape),
                  _const_spec(bdq.shape), _const_spec(bdk.shape), _const_spec(bdi.shape)],
        out_specs=[pl.BlockSpec((1, nsub, ATTN_KV_HEADS, HEAD_DIM, ATTN_REP * tq),
                                lambda bi, j: (bi, j, 0, 0, 0)),
                   pl.BlockSpec((1, ATTN_KV_HEADS, tt, HEAD_DIM), lambda bi, j: (bi, 0, j, 0)),
                   pl.BlockSpec((1, ATTN_KV_HEADS, nv, HEAD_DIM, LANES), lambda bi, j: (bi, 0, j, 0, 0)),
                   pl.BlockSpec((1, nsub, IDX_HEADS, IDX_DIM, tq), lambda bi, j: (bi, j, 0, 0, 0)),
                   pl.BlockSpec((1, tt, IDX_DIM), lambda bi, j: (bi, j, 0)),
                   pl.BlockSpec((1, nsub, IDX_HEADS, tq), lambda bi, j: (bi, j, 0, 0))],
        out_shape=[jax.ShapeDtypeStruct((b, nq, ATTN_KV_HEADS, HEAD_DIM, ATTN_REP * tq), BF16),
                   jax.ShapeDtypeStruct((b, ATTN_KV_HEADS, s, HEAD_DIM), BF16),
                   jax.ShapeDtypeStruct((b, ATTN_KV_HEADS, s // LANES, HEAD_DIM, LANES), BF16),
                   jax.ShapeDtypeStruct((b, nq, IDX_HEADS, IDX_DIM, tq), BF16),
                   jax.ShapeDtypeStruct((b, s, IDX_DIM), BF16),
                   jax.ShapeDtypeStruct((b, nq, IDX_HEADS, tq), F32)],
        compiler_params=_cparams("parallel", "parallel"),
        name="attn_prep",
    )(z3, *tables, qg, kg, kig, bdq, bdk, bdi)


def _order_key(x):
    return x ^ (lax.shift_right_arithmetic(x, 31) & 0x7FFFFFFF)


def _f32_to_key(v):
    return _order_key(lax.bitcast_convert_type(v, jnp.int32))


def _key_to_f32(key):
    return lax.bitcast_convert_type(_order_key(key), F32)


def _dsa_kernel(qi_ref, w_ref, q_ref, ki_ref, k_ref, v_ref, o_ref,
                sc_ref, thr_ref, m_ref, acc_ref, *, tq, ck, topk, interp_rounds):
    qb = pl.program_id(1)
    n_chunks = lax.div(qb * tq, jnp.int32(ck)) + 1
    nr = ck // SUBLANES
    shape3 = (nr, SUBLANES, tq)
    q_pos = qb * tq + lax.broadcasted_iota(jnp.int32, (1, tq), 1)
    q_pos_b = jnp.broadcast_to(q_pos, (SUBLANES, tq))[None]
    key_id = (lax.broadcasted_iota(jnp.int32, shape3, 0) * SUBLANES
              + lax.broadcasted_iota(jnp.int32, shape3, 1))

    w = w_ref[0, 0]
    wb = [jnp.broadcast_to(w[h:h + 1, :], (SUBLANES, tq))[None] for h in range(IDX_HEADS)]

    def score_chunk(c, carry, diag):
        smax, smin = carry
        off = pl.multiple_of(c * ck, ck)
        kic = ki_ref[0, pl.ds(off, ck), :]
        sc = None
        for h in range(IDX_HEADS):
            lg = jnp.dot(kic, qi_ref[0, 0, h], preferred_element_type=F32)
            t = jnp.maximum(lg.reshape(shape3), 0.0) * wb[h]
            sc = t if sc is None else sc + t
        kid = key_id + off
        key = jnp.where(sc == 0.0, -2 - kid, _f32_to_key(sc))
        if diag:
            adm = kid <= q_pos_b
            smax = jnp.maximum(smax, jnp.max(jnp.where(adm, sc, -jnp.inf), axis=0))
            smin = jnp.minimum(smin, jnp.min(jnp.where(adm, sc, jnp.inf), axis=0))
            key = jnp.where(adm, key, INT32_MIN)
        else:
            smax = jnp.maximum(smax, jnp.max(sc, axis=0))
            smin = jnp.minimum(smin, jnp.min(sc, axis=0))
        sc_ref[c] = key.reshape(ck, tq)
        return smax, smin

    ext = (jnp.full((SUBLANES, tq), -jnp.inf, F32), jnp.full((SUBLANES, tq), jnp.inf, F32))
    ext = lax.fori_loop(0, n_chunks - 1, functools.partial(score_chunk, diag=False), ext)
    smax, smin = score_chunk(n_chunks - 1, ext, True)
    kmax = _f32_to_key(jnp.max(smax, axis=0, keepdims=True))
    kmin = _f32_to_key(jnp.min(smin, axis=0, keepdims=True))

    def count(pred):
        def body(c, acc):
            hits = pred(sc_ref[c].reshape(shape3), c).astype(jnp.int32)
            return acc + jnp.sum(hits.reshape(nr // COUNT_CHAINS, COUNT_CHAINS, SUBLANES, tq), axis=0)
        acc = lax.fori_loop(0, n_chunks, body, jnp.zeros((COUNT_CHAINS, SUBLANES, tq), jnp.int32))
        return jnp.sum(jnp.sum(acc, axis=0), axis=0, keepdims=True)

    def count_ge(kx):
        kx_b = jnp.broadcast_to(kx, (SUBLANES, tq))[None]
        return count(lambda x, c: x >= kx_b)

    n_adm = q_pos + 1
    c_pos = count_ge(jnp.zeros((1, tq), jnp.int32))
    c_nn = count_ge(jnp.full((1, tq), ZERO_BAND_LO, jnp.int32))
    band = jnp.logical_and(c_pos < topk, c_nn > topk)
    pos = c_pos > topk
    all_in = n_adm <= topk
    done0 = jnp.logical_or(all_in, jnp.logical_or(c_pos == topk, c_nn == topk))
    thr0 = jnp.where(all_in, INT32_MIN + 1, jnp.where(c_pos == topk, 0, ZERO_BAND_LO))
    lo0 = jnp.where(band, -2 - n_chunks * ck, jnp.where(pos, 0, kmin))
    hi0 = jnp.where(band, -1, jnp.where(pos, kmax + 1, ZERO_BAND_LO))
    clo0 = jnp.where(band, c_nn, jnp.where(pos, c_pos, n_adm)).astype(F32)
    chi0 = jnp.where(band, c_pos, jnp.where(pos, 0, c_nn)).astype(F32)
    log_k = math.log(topk - 0.5)

    def as_value(key):
        return jnp.where(band, key.astype(F32), _key_to_f32(key))

    def search_step(st, how):
        lo, hi, c_lo, c_hi, done, thr = st
        span = hi - lo
        if how == "key_mid":
            kx = lo + lax.shift_right_logical(span, 1)
        else:
            if how == "value_mid":
                frac = 0.5
            else:
                lin = (c_lo - (topk - 0.5)) / (c_lo - c_hi)
                lg_lo = jnp.log(c_lo)
                lg = (lg_lo - log_k) / (lg_lo - jnp.log(jnp.maximum(c_hi, 0.5)))
                frac = jnp.clip(jnp.where(band, lin, lg), 0.0, 1.0)
            lo_v, hi_v = as_value(lo), as_value(hi)
            x = lo_v + (hi_v - lo_v) * frac
            kx = jnp.where(band, jnp.floor(x).astype(jnp.int32), _f32_to_key(x))
        kx = jnp.minimum(jnp.maximum(kx, lo + 1), hi - 1)
        cnt = count_ge(kx)
        live = jnp.logical_and(done == 0, span >= 2)
        hit = jnp.logical_and(live, cnt == topk)
        up = jnp.logical_and(live, cnt > topk)
        down = jnp.logical_and(live, cnt < topk)
        cf = cnt.astype(F32)
        return (jnp.where(up, kx, lo), jnp.where(down, kx, hi),
                jnp.where(up, cf, c_lo), jnp.where(down, cf, c_hi),
                jnp.where(hit, 1, done), jnp.where(hit, kx, thr))

    def unsettled(st):
        lo, hi, _, _, done, _ = st
        return jnp.max(jnp.where(jnp.logical_and(done == 0, hi - lo >= 2), 1, 0)) > 0

    def interp_cond(st):
        return jnp.logical_and(st[0] < interp_rounds, unsettled(st[1]))

    def interp_round(st):
        inner = search_step(search_step(search_step(st[1], "interp"), "interp"), "value_mid")
        return st[0] + 1, inner

    st = lax.while_loop(interp_cond, interp_round,
                        (jnp.int32(0), (lo0, hi0, clo0, chi0, done0.astype(jnp.int32), thr0)))[1]
    st = lax.while_loop(unsettled, lambda s: search_step(search_step(s, "key_mid"), "key_mid"), st)
    lo, _, _, c_gt, done, thr = st
    tied = done == 0
    thr = jnp.where(tied, lo, thr)
    thr_ref[...] = thr

    @pl.when(jnp.max(tied.astype(jnp.int32)) > 0)
    def _():
        want = topk - c_gt.astype(jnp.int32)
        thr_b = jnp.broadcast_to(thr, (SUBLANES, tq))[None]

        def bisect_index(_, carry):
            jlo, jhi = carry
            mid = jlo + lax.shift_right_logical(jhi - jlo, 1)
            mid_b = jnp.broadcast_to(mid, (SUBLANES, tq))[None]
            cnt = count(lambda x, c: jnp.logical_and(x == thr_b, key_id + c * ck <= mid_b))
            ge = cnt >= want
            return jnp.where(ge, jlo, mid), jnp.where(ge, mid, jhi)

        n_idx_steps = int(math.ceil(math.log2(sc_ref.shape[0] * ck + 1)))
        jinit = (jnp.full((1, tq), -1, jnp.int32), (n_chunks * ck - 1) + jnp.zeros((1, tq), jnp.int32))
        _, cut = lax.fori_loop(0, n_idx_steps, bisect_index, jinit)
        cut_b = jnp.broadcast_to(jnp.where(tied, cut, INT32_MAX), (SUBLANES, tq))[None]

        def drop(c, carry):
            x = sc_ref[c].reshape(shape3)
            gone = jnp.logical_and(x == thr_b, key_id + c * ck > cut_b)
            sc_ref[c] = jnp.where(gone, INT32_MIN, x).reshape(ck, tq)
            return carry

        lax.fori_loop(0, n_chunks, drop, 0)

    thr_b = jnp.broadcast_to(thr_ref[...], (SUBLANES, tq))[None]
    m_ref[...] = jnp.full(m_ref.shape, NEG_BIG, F32)
    acc_ref[...] = jnp.zeros(acc_ref.shape, F32)
    ak = min(ATTN_SUB_KEYS, ck)
    ones_rows = jnp.ones((acc_ref.shape[1] - HEAD_DIM, ak), BF16)
    vblocks = ck // LANES

    def attend_chunk(c, carry):
        off = pl.multiple_of(c * ck, ck)
        for sub in range(ck // ak):
            r0 = sub * ak
            sel = sc_ref[c, r0:r0 + ak, :].reshape(ak // SUBLANES, SUBLANES, tq) >= thr_b
            bias = jnp.where(sel, 0.0, NEG_BIG).reshape(ak, tq)
            for g in range(ATTN_KV_HEADS):
                kc = k_ref[0, g, pl.ds(off + r0, ak), :]
                vt = jnp.concatenate([v_ref[0, g, c * vblocks + sub * (ak // LANES) + i]
                                      for i in range(ak // LANES)], axis=1)
                vext = jnp.concatenate([vt, ones_rows], axis=0)
                for hp in range(ATTN_REP // HEADS_PER_DOT):
                    l0, l1 = hp * HEADS_PER_DOT * tq, (hp + 1) * HEADS_PER_DOT * tq
                    s = jnp.dot(kc, q_ref[0, 0, g, :, l0:l1], preferred_element_type=F32)
                    ps, alphas = [], []
                    for r in range(HEADS_PER_DOT):
                        h = g * ATTN_REP + hp * HEADS_PER_DOT + r
                        sh = s[:, r * tq:(r + 1) * tq] + bias
                        m_old = m_ref[h]
                        m_new = jnp.maximum(m_old, jnp.max(
                            jnp.max(sh.reshape(ak // SUBLANES, SUBLANES, tq), axis=0), axis=0, keepdims=True))
                        ps.append(jnp.exp2(sh - m_new).astype(BF16))
                        alphas.append(jnp.exp2(m_old - m_new))
                        m_ref[h] = m_new
                    acc_ref[g, :, l0:l1] = (jnp.concatenate(alphas, axis=1) * acc_ref[g, :, l0:l1]
                                            + jnp.dot(vext, jnp.concatenate(ps, axis=1),
                                                      preferred_element_type=F32))
        return carry

    lax.fori_loop(0, n_chunks, attend_chunk, 0)

    outs = []
    for g in range(ATTN_KV_HEADS):
        a = acc_ref[g]
        o = a[0:HEAD_DIM] / a[HEAD_DIM:HEAD_DIM + 1]
        outs += [o[:, r * tq:(r + 1) * tq] for r in range(ATTN_REP)]
    o_ref[0] = jnp.concatenate(outs, axis=0).T.astype(o_ref.dtype)


def _dsa(qt, kh, vt, qit, kir, wt, tq, ck):
    b, _, s, _ = kh.shape
    assert ck % tq == 0, "only the last causal chunk may hold inadmissible keys"
    topk = min(TOPK_MAX, s // 4)
    kernel = functools.partial(_dsa_kernel, tq=tq, ck=ck, topk=topk, interp_rounds=7)
    return pl.pallas_call(
        kernel,
        grid=(b, s // tq),
        in_specs=[pl.BlockSpec((1, 1, IDX_HEADS, IDX_DIM, tq), lambda bi, j: (bi, j, 0, 0, 0)),
                  pl.BlockSpec((1, 1, IDX_HEADS, tq), lambda bi, j: (bi, j, 0, 0)),
                  pl.BlockSpec((1, 1, ATTN_KV_HEADS, HEAD_DIM, ATTN_REP * tq), lambda bi, j: (bi, j, 0, 0, 0)),
                  pl.BlockSpec((1, s, IDX_DIM), lambda bi, j: (bi, 0, 0)),
                  pl.BlockSpec((1, ATTN_KV_HEADS, s, HEAD_DIM), lambda bi, j: (bi, 0, 0, 0)),
                  pl.BlockSpec((1, ATTN_KV_HEADS, s // LANES, HEAD_DIM, LANES),
                               lambda bi, j: (bi, 0, 0, 0, 0))],
        out_specs=pl.BlockSpec((1, tq, ATTN_WIDTH), lambda bi, j: (bi, j, 0)),
        out_shape=jax.ShapeDtypeStruct((b, s, ATTN_WIDTH), BF16),
        scratch_shapes=[pltpu.VMEM((s // ck, ck, tq), jnp.int32),
                        pltpu.VMEM((1, tq), jnp.int32),
                        pltpu.VMEM((ATTN_HEADS, 1, tq), F32),
                        pltpu.VMEM((ATTN_KV_HEADS, HEAD_DIM + 2 * SUBLANES, ATTN_REP * tq), F32)],
        compiler_params=_cparams("parallel", "arbitrary"),
        name="dsa_mixer",
    )(qit, wt, qt, kir, kh, vt)


def _gelu_tanh(y):
    return 0.5 * y * (1.0 + jnp.tanh(math.sqrt(2.0 / math.pi) * (y + 0.044715 * (y * y * y))))


def _s5_kernel(u_ref, wb_ref, ar_ref, ai_ref, wc_ref, d_ref, gw_ref, gb_ref, o_ref,
               x_ref, st_ref, *, tt, nb):
    @pl.when(pl.program_id(0) == 0)
    def _():
        st_ref[...] = jnp.zeros(st_ref.shape, F32)

    u = u_ref[...]
    x_ref[...] = jnp.dot(u.astype(BF16), wb_ref[...], preferred_element_type=F32)
    ar = jnp.broadcast_to(ar_ref[...], (nb, S5_LANES))
    ai = jnp.broadcast_to(ai_ref[...], (nb, S5_LANES))

    def step(t, carry):
        xr, xi = carry
        r0 = pl.multiple_of(t * nb, nb)
        nxr = ar * xr - ai * xi + x_ref[pl.ds(r0, nb), 0:S5_LANES]
        nxi = ar * xi + ai * xr + x_ref[pl.ds(r0, nb), S5_LANES:2 * S5_LANES]
        x_ref[pl.ds(r0, nb), 0:S5_LANES] = nxr
        x_ref[pl.ds(r0, nb), S5_LANES:2 * S5_LANES] = nxi
        return nxr, nxi

    xr, xi = lax.fori_loop(0, tt, step, (st_ref[0:nb, :], st_ref[nb:2 * nb, :]), unroll=4)
    st_ref[0:nb, :] = xr
    st_ref[nb:2 * nb, :] = xi

    y = jnp.dot(x_ref[...].astype(BF16), wc_ref[...], preferred_element_type=F32)
    y = _gelu_tanh(y + d_ref[...] * u)
    gate = jnp.dot(y.astype(BF16), gw_ref[...], preferred_element_type=F32) + gb_ref[...]
    o_ref[...] = (y * jax.nn.sigmoid(gate)).astype(o_ref.dtype)


def _s5(u_t, wb, ar, ai, wc, d, gw, gb, nb, tt):
    n = u_t.shape[0]
    rows = nb * tt
    kernel = functools.partial(_s5_kernel, tt=tt, nb=nb)
    return pl.pallas_call(
        kernel,
        grid=(n // rows,),
        in_specs=[pl.BlockSpec((rows, S5_WIDTH), lambda i: (i, 0)),
                  _const_spec(wb.shape), _const_spec(ar.shape), _const_spec(ai.shape),
                  _const_spec(wc.shape), _const_spec(d.shape), _const_spec(gw.shape),
                  _const_spec(gb.shape)],
        out_specs=pl.BlockSpec((rows, S5_WIDTH), lambda i: (i, 0)),
        out_shape=jax.ShapeDtypeStruct((n, S5_WIDTH), BF16),
        scratch_shapes=[pltpu.VMEM((rows, 2 * S5_LANES), F32),
                        pltpu.VMEM((2 * nb, S5_LANES), F32)],
        compiler_params=_cparams("arbitrary"),
        name="s5_mixer",
    )(u_t, wb, ar, ai, wc, d, gw, gb)


_NT = (((1,), (1,)), ((), ()))
_TN = (((0,), (0,)), ((), ()))
_HI = lax.Precision.HIGHEST


def _ssd_kernel(z_ref, cw_ref, cb_ref, dtb_ref, a_ref, dsk_ref, ng_ref, tri_ref, trit_ref,
                o_ref, ext_ref, st_ref, *, cl):
    @pl.when(pl.program_id(1) == 0)
    def _():
        st_ref[...] = jnp.zeros(st_ref.shape, F32)
        ext_ref[0:SUBLANES, :] = jnp.zeros((SUBLANES, SSD_XBC), F32)

    tile = z_ref[0]
    zg = tile[:, 0:SSD_WIDTH]
    ext_ref[SUBLANES:SUBLANES + cl, :] = tile[:, SSD_WIDTH:SSD_WIDTH + SSD_XBC]
    conv = cb_ref[...]
    for k in range(SSD_CONV):
        start = SUBLANES - (SSD_CONV - 1) + k
        conv = conv + cw_ref[k:k + 1, :] * ext_ref[start:start + cl, :]
    ext_ref[0:SUBLANES, :] = ext_ref[cl:cl + SUBLANES, :]
    xa = conv * jax.nn.sigmoid(conv)
    xs = xa[:, 0:SSD_WIDTH]
    bm = xa[:, SSD_WIDTH:SSD_WIDTH + SSD_NGROUPS * SSD_STATE]
    cm = xa[:, SSD_WIDTH + SSD_NGROUPS * SSD_STATE:SSD_XBC]

    dtx = tile[:, SSD_WIDTH + SSD_XBC:SSD_SLAB] + dtb_ref[...]
    dt = jnp.maximum(dtx, 0.0) + jnp.log1p(jnp.exp(-jnp.abs(dtx)))
    a_dt = dt * a_ref[...]
    acs = jnp.dot(tri_ref[...], a_dt, precision=_HI, preferred_element_type=F32)
    acs_row = jnp.dot(a_dt.T[0:SUBLANES], trit_ref[...], precision=_HI, preferred_element_type=F32)
    causal = (lax.broadcasted_iota(jnp.int32, (cl, cl), 0)
              >= lax.broadcasted_iota(jnp.int32, (cl, cl), 1))

    cb_scores = []
    for g in range(SSD_NGROUPS):
        bg = bm[:, g * SSD_STATE:(g + 1) * SSD_STATE].astype(BF16)
        cg = cm[:, g * SSD_STATE:(g + 1) * SSD_STATE].astype(BF16)
        cb_scores.append(lax.dot_general(cg, bg, _NT, preferred_element_type=F32))

    rep = SSD_HEADS // SSD_NGROUPS
    ys = []
    for h in range(SSD_HEADS):
        g = h // rep
        acol = acs[:, h:h + 1]
        arow = acs_row[h:h + 1, :]
        alast = acs[cl - 1:cl, h:h + 1]
        decay = jnp.exp(jnp.where(causal, acol - arow, -jnp.inf))
        xs_h = xs[:, h * SSD_HEAD_DIM:(h + 1) * SSD_HEAD_DIM]
        xdt = (xs_h * dt[:, h:h + 1]).astype(BF16)
        y = jnp.dot((cb_scores[g] * decay).astype(BF16), xdt, preferred_element_type=F32)
        bg = bm[:, g * SSD_STATE:(g + 1) * SSD_STATE]
        cg = cm[:, g * SSD_STATE:(g + 1) * SSD_STATE].astype(BF16)
        prev = st_ref[h]
        y = y + jnp.dot(cg, prev.astype(BF16), preferred_element_type=F32) * jnp.exp(acol)
        bdec = (bg * jnp.exp(alast - acol)).astype(BF16)
        st_ref[h] = prev * jnp.exp(alast) + lax.dot_general(bdec, xdt, _TN, preferred_element_type=F32)
        ys.append(y + dsk_ref[:, h:h + 1] * xs_h)

    y = jnp.concatenate(ys, axis=-1) * (zg * jax.nn.sigmoid(zg))
    gw = SSD_WIDTH // SSD_NGROUPS
    outs = []
    for g in range(SSD_NGROUPS):
        yg = y[:, g * gw:(g + 1) * gw]
        outs.append(yg * lax.rsqrt(jnp.mean(yg * yg, axis=-1, keepdims=True) + EPS))
    o_ref[0] = (jnp.concatenate(outs, axis=-1) * ng_ref[...]).astype(o_ref.dtype)


def _ssd(z3, cw, cb, dtb, a, dsk, ng, tri, trit, cl):
    b, s, _ = z3.shape
    kernel = functools.partial(_ssd_kernel, cl=cl)
    return pl.pallas_call(
        kernel,
        grid=(b, s // cl),
        in_specs=[pl.BlockSpec((1, cl, SSD_SLAB), lambda bi, j: (bi, j, OFF_SSD // SSD_SLAB)),
                  _const_spec(cw.shape), _const_spec(cb.shape), _const_spec(dtb.shape),
                  _const_spec(a.shape), _const_spec(dsk.shape), _const_spec(ng.shape),
                  _const_spec(tri.shape), _const_spec(trit.shape)],
        out_specs=pl.BlockSpec((1, cl, SSD_WIDTH), lambda bi, j: (bi, j, 0)),
        out_shape=jax.ShapeDtypeStruct((b, s, SSD_WIDTH), BF16),
        scratch_shapes=[pltpu.VMEM((cl + 2 * SUBLANES, SSD_XBC), F32),
                        pltpu.VMEM((SSD_HEADS, SSD_STATE, SSD_HEAD_DIM), F32)],
        compiler_params=_cparams("parallel", "arbitrary"),
        name="ssd_mixer",
    )(z3, cw, cb, dtb, a, dsk, ng, tri, trit)


def _out_mlp_kernel(x_ref, a_ref, s_ref, m_ref, wo_ref, g_ref, wu_ref, wd_ref, o_ref, *, fc):
    wo = wo_ref
    mix = jnp.dot(a_ref[...], wo[0:ATTN_WIDTH, :], preferred_element_type=F32)
    mix = mix + jnp.dot(s_ref[...], wo[ATTN_WIDTH:ATTN_WIDTH + S5_WIDTH, :], preferred_element_type=F32)
    mix = mix + jnp.dot(m_ref[...], wo[ATTN_WIDTH + S5_WIDTH:, :], preferred_element_type=F32)
    x1 = x_ref[...] + mix
    ms = jnp.mean(x1 * x1, axis=-1, keepdims=True)
    h = (x1 * lax.rsqrt(ms + EPS) * g_ref[...]).astype(BF16)
    acc = x1
    for c in range(wu_ref.shape[1] // fc):
        up = jnp.maximum(jnp.dot(h, wu_ref[:, c * fc:(c + 1) * fc], preferred_element_type=F32), 0.0)
        acc = acc + jnp.dot((up * up).astype(BF16), wd_ref[c * fc:(c + 1) * fc, :],
                            preferred_element_type=F32)
    o_ref[...] = acc


def _out_mlp(x2d, attn, s5o, ssdo, wo, g, wu, wd, tm, fc):
    n, d = x2d.shape

    def row_spec(width):
        return pl.BlockSpec((tm, width), lambda i: (i, 0))

    def resident(shape):
        return pl.BlockSpec(shape, lambda i: (0, 0), pipeline_mode=pl.Buffered(1))

    return pl.pallas_call(
        functools.partial(_out_mlp_kernel, fc=fc),
        grid=(n // tm,),
        in_specs=[row_spec(d), row_spec(ATTN_WIDTH), row_spec(S5_WIDTH), row_spec(SSD_WIDTH),
                  resident(wo.shape), resident(g.shape), resident(wu.shape), resident(wd.shape)],
        out_specs=row_spec(d),
        out_shape=jax.ShapeDtypeStruct((n, d), F32),
        compiler_params=_cparams("parallel"),
        name="out_proj_mlp",
    )(x2d, attn, s5o, ssdo, wo, g, wu, wd)


def _pad_w_in(w_in_l):
    d = w_in_l.shape[0]
    sizes = (ATTN_WIDTH, KV_WIDTH, KV_WIDTH, IDX_HEADS * IDX_DIM, IDX_DIM, IDX_HEADS,
             S5_WIDTH, SSD_WIDTH, SSD_XBC, SSD_HEADS)
    offs = np.concatenate([[0], np.cumsum(sizes)])
    dst = (OFF_Q, OFF_K, OFF_V, OFF_QI, OFF_KW, OFF_KW + IDX_DIM,
           OFF_S5, OFF_SSD, OFF_SSD + SSD_WIDTH, OFF_SSD + SSD_WIDTH + SSD_XBC)
    out = jnp.zeros((d, N_IN_PAD), F32)
    for i, n in enumerate(sizes):
        out = out.at[:, dst[i]:dst[i] + n].set(w_in_l[:, offs[i]:offs[i] + n])
    return out.astype(BF16)


def _rope_tables(s):
    pos = jnp.arange(s, dtype=F32)
    inv_freq = ROPE_THETA ** (-jnp.arange(0, ROPE_DIM, 2, dtype=F32) / ROPE_DIM)
    ang = pos[:, None] * inv_freq[None, :]
    cos, sin = jnp.cos(ang), jnp.sin(ang)
    zeros = jnp.zeros((s, ROPE_HALF), F32)
    rest = HEAD_DIM - ROPE_DIM
    c = jnp.concatenate([cos, cos, jnp.ones((s, rest), F32)], axis=-1)
    sa = jnp.concatenate([-sin, zeros, jnp.zeros((s, rest), F32)], axis=-1)
    sb = jnp.concatenate([zeros, sin, jnp.zeros((s, rest), F32)], axis=-1)
    return c, sa, sb


def _block_diag_mean(width, seg, active=None):
    idx = np.arange(width)
    m = (idx[:, None] // seg == idx[None, :] // seg).astype(np.float32) / seg
    if active is not None:
        m = m * (idx[:, None] < active) * (idx[None, :] < active)
    return jnp.asarray(m, BF16)


def _pad_lanes(v, fill=0.0):
    v = v.reshape(1, -1).astype(F32)
    return jnp.pad(v, ((0, 0), (0, LANES - v.shape[1])), constant_values=fill)


def _s5_params(lam_re, lam_im, log_step, b_re, b_im, c_re, c_im):
    step = jnp.exp(log_step.astype(F32))[:, None]
    lr, li = lam_re.astype(F32), lam_im.astype(F32)
    mag = jnp.exp(lr * step)
    ab_re = mag * jnp.cos(li * step)
    ab_im = mag * jnp.sin(li * step)
    den = lr * lr + li * li
    cr = ((ab_re - 1.0) * lr + ab_im * li) / den
    ci = (ab_im * lr - (ab_re - 1.0) * li) / den
    bb_re = cr[..., None] * b_re - ci[..., None] * b_im
    bb_im = cr[..., None] * b_im + ci[..., None] * b_re
    eye = jnp.eye(S5_GROUPS, dtype=F32)

    def in_bd(bb):
        return jnp.einsum('gph,gk->ghkp', bb, eye).reshape(S5_WIDTH, S5_LANES)

    def out_bd(cc):
        return jnp.einsum('ghp,gk->gpkh', cc, eye).reshape(S5_LANES, S5_WIDTH)

    wb = jnp.concatenate([in_bd(bb_re), in_bd(bb_im)], axis=1).astype(BF16)
    wc = jnp.concatenate([out_bd(c_re.astype(F32)), -out_bd(c_im.astype(F32))], axis=0).astype(BF16)
    return wb, ab_re.reshape(1, S5_LANES), ab_im.reshape(1, S5_LANES), wc


def _tile(n, pref):
    return pref if n % pref == 0 else n


def kernel(x, norm_mix_g, w_in, attn_q_norm_g, attn_k_norm_g, idx_k_norm_g, s5_lambda_re, s5_lambda_im, s5_log_step, s5_b_re, s5_b_im, s5_c_re, s5_c_im, s5_d, s5_glu_w, s5_glu_b, ssd_conv_w, ssd_conv_b, ssd_dt_bias, ssd_a_log, ssd_d, ssd_norm_g, w_out, norm_mlp_g, w_up, w_down):
    b, s, d = x.shape
    n = b * s
    depth = w_in.shape[0]
    assert b == SUBLANES, "the S5 scan keeps one sequence per sublane"
    tm = _tile(n, 512)
    tt = _tile(s, 512)
    tq = _tile(s, 512)
    ck = _tile(s, 512)
    cl = _tile(s, 256)
    ts5 = _tile(s, 64)

    tables = _rope_tables(s)
    bdq = _block_diag_mean(ATTN_WIDTH, HEAD_DIM)
    bdk = _block_diag_mean(KV_WIDTH, HEAD_DIM)
    bdi = _block_diag_mean(LANES, IDX_DIM, active=IDX_DIM)
    tri = jnp.asarray(np.tril(np.ones((cl, cl), np.float32)))
    trit = jnp.asarray(np.triu(np.ones((cl, cl), np.float32)))

    x2d = x.reshape(n, d)
    for l in range(depth):
        z = _norm_matmul(x2d, norm_mix_g[l].reshape(1, d), _pad_w_in(w_in[l]), tm)
        z3 = z.reshape(b, s, N_IN_PAD)

        qg = jnp.tile(attn_q_norm_g[l].astype(F32), ATTN_HEADS).reshape(1, ATTN_WIDTH)
        kg = jnp.tile(attn_k_norm_g[l].astype(F32), ATTN_KV_HEADS).reshape(1, KV_WIDTH)
        kig = _pad_lanes(idx_k_norm_g[l])
        qt, kh, vt, qit, kir, wt = _prep(z3, tables, qg, kg, kig, bdq, bdk, bdi, tt, tq)
        attn = _dsa(qt, kh, vt, qit, kir, wt, tq, ck)

        wb, ar, ai, wc = _s5_params(s5_lambda_re[l], s5_lambda_im[l], s5_log_step[l],
                                    s5_b_re[l], s5_b_im[l], s5_c_re[l], s5_c_im[l])
        u_t = z3[:, :, OFF_S5:OFF_S5 + S5_WIDTH].transpose(1, 0, 2).reshape(n, S5_WIDTH)
        s5_t = _s5(u_t, wb, ar, ai, wc, s5_d[l].reshape(1, S5_WIDTH).astype(F32),
                   s5_glu_w[l].astype(BF16), s5_glu_b[l].reshape(1, S5_WIDTH).astype(F32), b, ts5)
        s5o = s5_t.reshape(s, b, S5_WIDTH).transpose(1, 0, 2).reshape(n, S5_WIDTH)

        ssdo = _ssd(z3, ssd_conv_w[l].astype(F32), ssd_conv_b[l].reshape(1, SSD_XBC).astype(F32),
                    _pad_lanes(ssd_dt_bias[l]), _pad_lanes(-jnp.exp(ssd_a_log[l].astype(F32))),
                    _pad_lanes(ssd_d[l]), ssd_norm_g[l].reshape(1, SSD_WIDTH).astype(F32),
                    tri, trit, cl)

        x2d = _out_mlp(x2d, attn.reshape(n, ATTN_WIDTH), s5o, ssdo.reshape(n, SSD_WIDTH),
                       w_out[l].astype(BF16), norm_mlp_g[l].reshape(1, d).astype(F32),
                       w_up[l].astype(BF16), w_down[l].astype(BF16), tm, 1024)
    return x2d.reshape(b, s, d)
```

```python
import functools
import math

import jax
import jax.numpy as jnp
import numpy as np
from jax import lax
from jax.experimental import pallas as pl
from jax.experimental.pallas import tpu as pltpu

F32 = jnp.float32
BF16 = jnp.bfloat16

HEAD_DIM = 64
ATTN_HEADS = 8
ATTN_KV_HEADS = 2
ATTN_REP = ATTN_HEADS // ATTN_KV_HEADS
ATTN_WIDTH = ATTN_HEADS * HEAD_DIM
KV_WIDTH = ATTN_KV_HEADS * HEAD_DIM
ATTN_SCALE = HEAD_DIM ** -0.5
ROPE_DIM = HEAD_DIM // 4
ROPE_HALF = ROPE_DIM // 2
ROPE_THETA = 500000.0
IDX_HEADS = 8
IDX_DIM = HEAD_DIM
IDX_W_SCALE = (IDX_HEADS ** -0.5) * (IDX_DIM ** -0.5)
TOPK_MAX = 256
S5_WIDTH = 256
S5_GROUP_CH = 16
S5_GROUPS = S5_WIDTH // S5_GROUP_CH
S5_STATE = 64
S5_LANES = S5_GROUPS * S5_STATE
SSD_WIDTH = 256
SSD_HEAD_DIM = 64
SSD_HEADS = SSD_WIDTH // SSD_HEAD_DIM
SSD_NGROUPS = 2
SSD_STATE = 64
SSD_CONV = 4
SSD_XBC = SSD_WIDTH + 2 * SSD_NGROUPS * SSD_STATE
EPS = 1e-6
LANES = 128
SUBLANES = 8

ATTN_SLAB = 1536
OFF_Q, OFF_K, OFF_V, OFF_QI, OFF_KW = 0, 512, 640, 768, 1280
SSD_SLAB = 896
OFF_SSD = ATTN_SLAB
OFF_S5 = OFF_SSD + SSD_SLAB
N_IN_PAD = OFF_S5 + S5_WIDTH
REST_WIDTH = N_IN_PAD - ATTN_SLAB

INT32_MIN = -2 ** 31
INT32_MAX = 2 ** 31 - 1
ZERO_BAND_LO = -2 ** 23
LOG2E = math.log2(math.e)
NEG_BIG = -1e30
COUNT_CHAINS = 4
ATTN_SUB_KEYS = 512
HEADS_PER_DOT = 4
VMEM_LIMIT = 56 * 1024 * 1024


def _cparams(*sem):
    return pltpu.CompilerParams(dimension_semantics=sem, vmem_limit_bytes=VMEM_LIMIT)


def _const_spec(shape):
    nd = len(shape)
    return pl.BlockSpec(shape, lambda *_: (0,) * nd)


def _segment_mean(sq, bd_ref):
    hi = sq.astype(BF16)
    lo = (sq - hi.astype(F32)).astype(BF16)
    bd = bd_ref[...]
    return (jnp.dot(hi, bd, preferred_element_type=F32)
            + jnp.dot(lo, bd, preferred_element_type=F32))


def _rope(x, c, sa, sb):
    w = x.shape[-1]
    reps = w // HEAD_DIM
    c, sa, sb = (jnp.tile(t, (1, reps)) if reps > 1 else t for t in (c, sa, sb))
    return x * c + pltpu.roll(x, w - ROPE_HALF, 1) * sa + pltpu.roll(x, ROPE_HALF, 1) * sb


def _in_proj_kernel(x_ref, g_ref, w_ref, c_ref, sa_ref, sb_ref, qg_ref, kg_ref, kig_ref,
                    bdq_ref, bdk_ref, bdi_ref,
                    rest_out, qt_out, k_out, vt_out, qit_out, ki_out, wt_out, *, tq):
    x = x_ref[...]
    ms = jnp.mean(x * x, axis=-1, keepdims=True)
    h = (x * lax.rsqrt(ms + EPS) * g_ref[...]).astype(BF16)
    z = jnp.dot(h, w_ref[...], preferred_element_type=F32)
    rest_out[...] = z[:, ATTN_SLAB:]
    tt = z.shape[0]
    c, sa, sb = c_ref[...], sa_ref[...], sb_ref[...]
    q = z[:, OFF_Q:OFF_Q + ATTN_WIDTH]
    qn = q * lax.rsqrt(_segment_mean(q * q, bdq_ref) + EPS) * qg_ref[...]
    qn = _rope(qn, c, sa, sb) * (ATTN_SCALE * LOG2E)
    k = z[:, OFF_K:OFF_K + KV_WIDTH]
    kn = k * lax.rsqrt(_segment_mean(k * k, bdk_ref) + EPS) * kg_ref[...]
    kn = _rope(kn, c, sa, sb)
    for h in range(ATTN_KV_HEADS):
        k_out[0, h] = kn[:, h * HEAD_DIM:(h + 1) * HEAD_DIM].astype(BF16)
    v = z[:, OFF_V:OFF_V + KV_WIDTH]
    qi = _rope(z[:, OFF_QI:OFF_QI + IDX_HEADS * IDX_DIM], c, sa, sb)
    kw = z[:, OFF_KW:OFF_KW + LANES]
    kin = kw * lax.rsqrt(_segment_mean(kw * kw, bdi_ref) + EPS) * kig_ref[...]
    kin = _rope(kin, c, sa, sb)
    ki_out[0] = kin[:, 0:IDX_DIM].astype(BF16)

    for i in range(tt // tq):
        rows = slice(i * tq, (i + 1) * tq)
        qt = qn[rows].T
        qit = qi[rows].T
        for h in range(ATTN_HEADS):
            g, r = divmod(h, ATTN_REP)
            qt_out[0, i, g, :, r * tq:(r + 1) * tq] = qt[h * HEAD_DIM:(h + 1) * HEAD_DIM].astype(BF16)
            qit_out[0, i, h] = qit[h * IDX_DIM:(h + 1) * IDX_DIM].astype(BF16)
        wt_out[0, i] = kw[rows].T[IDX_DIM:IDX_DIM + IDX_HEADS] * IDX_W_SCALE
    for i in range(tt // LANES):
        vt = v[i * LANES:(i + 1) * LANES].T
        for g in range(ATTN_KV_HEADS):
            vt_out[0, g, i] = vt[g * HEAD_DIM:(g + 1) * HEAD_DIM].astype(BF16)


def _in_proj(x3, g, w, tables, qg, kg, kig, bdq, bdk, bdi, tt, tq):
    b, s, d = x3.shape
    nq, nsub, nv = s // tq, tt // tq, tt // LANES
    tab_spec = pl.BlockSpec((tt, HEAD_DIM), lambda bi, j: (j, 0))
    return pl.pallas_call(
        functools.partial(_in_proj_kernel, tq=tq),
        grid=(b, s // tt),
        in_specs=[pl.BlockSpec((None, tt, d), lambda bi, j: (bi, j, 0)),
                  _const_spec(g.shape),
                  pl.BlockSpec(w.shape, lambda bi, j: (0, 0), pipeline_mode=pl.Buffered(1)),
                  tab_spec, tab_spec, tab_spec,
                  _const_spec(qg.shape), _const_spec(kg.shape), _const_spec(kig.shape),
                  _const_spec(bdq.shape), _const_spec(bdk.shape), _const_spec(bdi.shape)],
        out_specs=[pl.BlockSpec((None, tt, REST_WIDTH), lambda bi, j: (bi, j, 0)),
                   pl.BlockSpec((1, nsub, ATTN_KV_HEADS, HEAD_DIM, ATTN_REP * tq),
                                lambda bi, j: (bi, j, 0, 0, 0)),
                   pl.BlockSpec((1, ATTN_KV_HEADS, tt, HEAD_DIM), lambda bi, j: (bi, 0, j, 0)),
                   pl.BlockSpec((1, ATTN_KV_HEADS, nv, HEAD_DIM, LANES), lambda bi, j: (bi, 0, j, 0, 0)),
                   pl.BlockSpec((1, nsub, IDX_HEADS, IDX_DIM, tq), lambda bi, j: (bi, j, 0, 0, 0)),
                   pl.BlockSpec((1, tt, IDX_DIM), lambda bi, j: (bi, j, 0)),
                   pl.BlockSpec((1, nsub, IDX_HEADS, tq), lambda bi, j: (bi, j, 0, 0))],
        out_shape=[jax.ShapeDtypeStruct((b, s, REST_WIDTH), F32),
                   jax.ShapeDtypeStruct((b, nq, ATTN_KV_HEADS, HEAD_DIM, ATTN_REP * tq), BF16),
                   jax.ShapeDtypeStruct((b, ATTN_KV_HEADS, s, HEAD_DIM), BF16),
                   jax.ShapeDtypeStruct((b, ATTN_KV_HEADS, s // LANES, HEAD_DIM, LANES), BF16),
                   jax.ShapeDtypeStruct((b, nq, IDX_HEADS, IDX_DIM, tq), BF16),
                   jax.ShapeDtypeStruct((b, s, IDX_DIM), BF16),
                   jax.ShapeDtypeStruct((b, nq, IDX_HEADS, tq), F32)],
        compiler_params=_cparams("parallel", "parallel"),
        name="in_proj_prep",
    )(x3, g, w, *tables, qg, kg, kig, bdq, bdk, bdi)


def _order_key(x):
    return x ^ (lax.shift_right_arithmetic(x, 31) & 0x7FFFFFFF)


def _f32_to_key(v):
    return _order_key(lax.bitcast_convert_type(v, jnp.int32))


def _key_to_f32(key):
    return lax.bitcast_convert_type(_order_key(key), F32)


def _dsa_kernel(qi_ref, w_ref, q_ref, ki_ref, k_ref, v_ref, o_ref,
                sc_ref, thr_ref, m_ref, acc_ref, *, tq, ck, topk, interp_rounds):
    qb = pl.program_id(1)
    n_chunks = lax.div(qb * tq, jnp.int32(ck)) + 1
    nr = ck // SUBLANES
    shape3 = (nr, SUBLANES, tq)
    q_pos = qb * tq + lax.broadcasted_iota(jnp.int32, (1, tq), 1)
    q_pos_b = jnp.broadcast_to(q_pos, (SUBLANES, tq))[None]
    key_id = (lax.broadcasted_iota(jnp.int32, shape3, 0) * SUBLANES
              + lax.broadcasted_iota(jnp.int32, shape3, 1))

    w = w_ref[0, 0]
    wb = [jnp.broadcast_to(w[h:h + 1, :], (SUBLANES, tq))[None] for h in range(IDX_HEADS)]

    def score_chunk(c, carry, diag):
        smax, smin = carry
        off = pl.multiple_of(c * ck, ck)
        kic = ki_ref[0, pl.ds(off, ck), :]
        sc = None
        for h in range(IDX_HEADS):
            lg = jnp.dot(kic, qi_ref[0, 0, h], preferred_element_type=F32)
            t = jnp.maximum(lg.reshape(shape3), 0.0) * wb[h]
            sc = t if sc is None else sc + t
        kid = key_id + off
        key = jnp.where(sc == 0.0, -2 - kid, _f32_to_key(sc))
        if diag:
            adm = kid <= q_pos_b
            smax = jnp.maximum(smax, jnp.max(jnp.where(adm, sc, -jnp.inf), axis=0))
            smin = jnp.minimum(smin, jnp.min(jnp.where(adm, sc, jnp.inf), axis=0))
            key = jnp.where(adm, key, INT32_MIN)
        else:
            smax = jnp.maximum(smax, jnp.max(sc, axis=0))
            smin = jnp.minimum(smin, jnp.min(sc, axis=0))
        sc_ref[c] = key.reshape(ck, tq)
        return smax, smin

    ext = (jnp.full((SUBLANES, tq), -jnp.inf, F32), jnp.full((SUBLANES, tq), jnp.inf, F32))
    ext = lax.fori_loop(0, n_chunks - 1, functools.partial(score_chunk, diag=False), ext)
    smax, smin = score_chunk(n_chunks - 1, ext, True)
    kmax = _f32_to_key(jnp.max(smax, axis=0, keepdims=True))
    kmin = _f32_to_key(jnp.min(smin, axis=0, keepdims=True))

    def count(pred):
        def body(c, acc):
            hits = pred(sc_ref[c].reshape(shape3), c).astype(jnp.int32)
            return acc + jnp.sum(hits.reshape(nr // COUNT_CHAINS, COUNT_CHAINS, SUBLANES, tq), axis=0)
        acc = lax.fori_loop(0, n_chunks, body, jnp.zeros((COUNT_CHAINS, SUBLANES, tq), jnp.int32))
        return jnp.sum(jnp.sum(acc, axis=0), axis=0, keepdims=True)

    def count_ge(kx):
        kx_b = jnp.broadcast_to(kx, (SUBLANES, tq))[None]
        return count(lambda x, c: x >= kx_b)

    n_adm = q_pos + 1
    c_pos = count_ge(jnp.zeros((1, tq), jnp.int32))
    c_nn = count_ge(jnp.full((1, tq), ZERO_BAND_LO, jnp.int32))
    band = jnp.logical_and(c_pos < topk, c_nn > topk)
    pos = c_pos > topk
    all_in = n_adm <= topk
    done0 = jnp.logical_or(all_in, jnp.logical_or(c_pos == topk, c_nn == topk))
    thr0 = jnp.where(all_in, INT32_MIN + 1, jnp.where(c_pos == topk, 0, ZERO_BAND_LO))
    lo0 = jnp.where(band, -2 - n_chunks * ck, jnp.where(pos, 0, kmin))
    hi0 = jnp.where(band, -1, jnp.where(pos, kmax + 1, ZERO_BAND_LO))
    clo0 = jnp.where(band, c_nn, jnp.where(pos, c_pos, n_adm)).astype(F32)
    chi0 = jnp.where(band, c_pos, jnp.where(pos, 0, c_nn)).astype(F32)
    log_k = math.log(topk - 0.5)

    def as_value(key):
        return jnp.where(band, key.astype(F32), _key_to_f32(key))

    def search_step(st, how):
        lo, hi, c_lo, c_hi, done, thr = st
        span = hi - lo
        if how == "key_mid":
            kx = lo + lax.shift_right_logical(span, 1)
        else:
            if how == "value_mid":
                frac = 0.5
            else:
                lin = (c_lo - (topk - 0.5)) / (c_lo - c_hi)
                lg_lo = jnp.log(c_lo)
                lg = (lg_lo - log_k) / (lg_lo - jnp.log(jnp.maximum(c_hi, 0.5)))
                frac = jnp.clip(jnp.where(band, lin, lg), 0.0, 1.0)
            lo_v, hi_v = as_value(lo), as_value(hi)
            x = lo_v + (hi_v - lo_v) * frac
            kx = jnp.where(band, jnp.floor(x).astype(jnp.int32), _f32_to_key(x))
        kx = jnp.minimum(jnp.maximum(kx, lo + 1), hi - 1)
        cnt = count_ge(kx)
        live = jnp.logical_and(done == 0, span >= 2)
        hit = jnp.logical_and(live, cnt == topk)
        up = jnp.logical_and(live, cnt > topk)
        down = jnp.logical_and(live, cnt < topk)
        cf = cnt.astype(F32)
        return (jnp.where(up, kx, lo), jnp.where(down, kx, hi),
                jnp.where(up, cf, c_lo), jnp.where(down, cf, c_hi),
                jnp.where(hit, 1, done), jnp.where(hit, kx, thr))

    def unsettled(st):
        lo, hi, _, _, done, _ = st
        return jnp.max(jnp.where(jnp.logical_and(done == 0, hi - lo >= 2), 1, 0)) > 0

    def interp_cond(st):
        return jnp.logical_and(st[0] < interp_rounds, unsettled(st[1]))

    def interp_round(st):
        inner = search_step(search_step(search_step(st[1], "interp"), "interp"), "value_mid")
        return st[0] + 1, inner

    st = lax.while_loop(interp_cond, interp_round,
                        (jnp.int32(0), (lo0, hi0, clo0, chi0, done0.astype(jnp.int32), thr0)))[1]
    st = lax.while_loop(unsettled, lambda s: search_step(search_step(s, "key_mid"), "key_mid"), st)
    lo, _, _, c_gt, done, thr = st
    tied = done == 0
    thr = jnp.where(tied, lo, thr)
    thr_ref[...] = thr

    @pl.when(jnp.max(tied.astype(jnp.int32)) > 0)
    def _():
        want = topk - c_gt.astype(jnp.int32)
        thr_b = jnp.broadcast_to(thr, (SUBLANES, tq))[None]

        def bisect_index(_, carry):
            jlo, jhi = carry
            mid = jlo + lax.shift_right_logical(jhi - jlo, 1)
            mid_b = jnp.broadcast_to(mid, (SUBLANES, tq))[None]
            cnt = count(lambda x, c: jnp.logical_and(x == thr_b, key_id + c * ck <= mid_b))
            ge = cnt >= want
            return jnp.where(ge, jlo, mid), jnp.where(ge, mid, jhi)

        n_idx_steps = int(math.ceil(math.log2(sc_ref.shape[0] * ck + 1)))
        jinit = (jnp.full((1, tq), -1, jnp.int32), (n_chunks * ck - 1) + jnp.zeros((1, tq), jnp.int32))
        _, cut = lax.fori_loop(0, n_idx_steps, bisect_index, jinit)
        cut_b = jnp.broadcast_to(jnp.where(tied, cut, INT32_MAX), (SUBLANES, tq))[None]

        def drop(c, carry):
            x = sc_ref[c].reshape(shape3)
            gone = jnp.logical_and(x == thr_b, key_id + c * ck > cut_b)
            sc_ref[c] = jnp.where(gone, INT32_MIN, x).reshape(ck, tq)
            return carry

        lax.fori_loop(0, n_chunks, drop, 0)

    thr_b = jnp.broadcast_to(thr_ref[...], (SUBLANES, tq))[None]
    m_ref[...] = jnp.full(m_ref.shape, NEG_BIG, F32)
    acc_ref[...] = jnp.zeros(acc_ref.shape, F32)
    ak = min(ATTN_SUB_KEYS, ck)
    ones_rows = jnp.ones((acc_ref.shape[1] - HEAD_DIM, ak), BF16)
    vblocks = ck // LANES

    def attend_chunk(c, carry):
        off = pl.multiple_of(c * ck, ck)
        for sub in range(ck // ak):
            r0 = sub * ak
            sel = sc_ref[c, r0:r0 + ak, :].reshape(ak // SUBLANES, SUBLANES, tq) >= thr_b
            bias = jnp.where(sel, 0.0, NEG_BIG).reshape(ak, tq)
            for g in range(ATTN_KV_HEADS):
                kc = k_ref[0, g, pl.ds(off + r0, ak), :]
                vt = jnp.concatenate([v_ref[0, g, c * vblocks + sub * (ak // LANES) + i]
                                      for i in range(ak // LANES)], axis=1)
                vext = jnp.concatenate([vt, ones_rows], axis=0)
                for hp in range(ATTN_REP // HEADS_PER_DOT):
                    l0, l1 = hp * HEADS_PER_DOT * tq, (hp + 1) * HEADS_PER_DOT * tq
                    s = jnp.dot(kc, q_ref[0, 0, g, :, l0:l1], preferred_element_type=F32)
                    ps, alphas = [], []
                    for r in range(HEADS_PER_DOT):
                        h = g * ATTN_REP + hp * HEADS_PER_DOT + r
                        sh = s[:, r * tq:(r + 1) * tq] + bias
                        m_old = m_ref[h]
                        m_new = jnp.maximum(m_old, jnp.max(
                            jnp.max(sh.reshape(ak // SUBLANES, SUBLANES, tq), axis=0), axis=0, keepdims=True))
                        ps.append(jnp.exp2(sh - m_new).astype(BF16))
                        alphas.append(jnp.exp2(m_old - m_new))
                        m_ref[h] = m_new
                    acc_ref[g, :, l0:l1] = (jnp.concatenate(alphas, axis=1) * acc_ref[g, :, l0:l1]
                                            + jnp.dot(vext, jnp.concatenate(ps, axis=1),
                                                      preferred_element_type=F32))
        return carry

    lax.fori_loop(0, n_chunks, attend_chunk, 0)

    outs = []
    for g in range(ATTN_KV_HEADS):
        a = acc_ref[g]
        o = a[0:HEAD_DIM] / a[HEAD_DIM:HEAD_DIM + 1]
        outs += [o[:, r * tq:(r + 1) * tq] for r in range(ATTN_REP)]
    o_ref[0] = jnp.concatenate(outs, axis=0).T.astype(o_ref.dtype)


def _dsa(qt, kh, vt, qit, kir, wt, tq, ck):
    b, _, s, _ = kh.shape
    assert ck % tq == 0, "only the last causal chunk may hold inadmissible keys"
    topk = min(TOPK_MAX, s // 4)
    kernel = functools.partial(_dsa_kernel, tq=tq, ck=ck, topk=topk, interp_rounds=7)
    return pl.pallas_call(
        kernel,
        grid=(b, s // tq),
        in_specs=[pl.BlockSpec((1, 1, IDX_HEADS, IDX_DIM, tq), lambda bi, j: (bi, j, 0, 0, 0)),
                  pl.BlockSpec((1, 1, IDX_HEADS, tq), lambda bi, j: (bi, j, 0, 0)),
                  pl.BlockSpec((1, 1, ATTN_KV_HEADS, HEAD_DIM, ATTN_REP * tq), lambda bi, j: (bi, j, 0, 0, 0)),
                  pl.BlockSpec((1, s, IDX_DIM), lambda bi, j: (bi, 0, 0)),
                  pl.BlockSpec((1, ATTN_KV_HEADS, s, HEAD_DIM), lambda bi, j: (bi, 0, 0, 0)),
                  pl.BlockSpec((1, ATTN_KV_HEADS, s // LANES, HEAD_DIM, LANES),
                               lambda bi, j: (bi, 0, 0, 0, 0))],
        out_specs=pl.BlockSpec((1, tq, ATTN_WIDTH), lambda bi, j: (bi, j, 0)),
        out_shape=jax.ShapeDtypeStruct((b, s, ATTN_WIDTH), BF16),
        scratch_shapes=[pltpu.VMEM((s // ck, ck, tq), jnp.int32),
                        pltpu.VMEM((1, tq), jnp.int32),
                        pltpu.VMEM((ATTN_HEADS, 1, tq), F32),
                        pltpu.VMEM((ATTN_KV_HEADS, HEAD_DIM + 2 * SUBLANES, ATTN_REP * tq), F32)],
        compiler_params=_cparams("parallel", "arbitrary"),
        name="dsa_mixer",
    )(qit, wt, qt, kir, kh, vt)


def _gelu_tanh(y):
    return 0.5 * y * (1.0 + jnp.tanh(math.sqrt(2.0 / math.pi) * (y + 0.044715 * (y * y * y))))


def _s5_kernel(u_ref, wb_ref, ar_ref, ai_ref, wc_ref, d_ref, gw_ref, gb_ref, o_ref,
               x_ref, st_ref, *, tt, nb):
    @pl.when(pl.program_id(0) == 0)
    def _():
        st_ref[...] = jnp.zeros(st_ref.shape, F32)

    u = u_ref[...]
    x_ref[...] = jnp.dot(u.astype(BF16), wb_ref[...], preferred_element_type=F32)
    ar = jnp.broadcast_to(ar_ref[...], (nb, S5_LANES))
    ai = jnp.broadcast_to(ai_ref[...], (nb, S5_LANES))

    def step(t, carry):
        xr, xi = carry
        r0 = pl.multiple_of(t * nb, nb)
        nxr = ar * xr - ai * xi + x_ref[pl.ds(r0, nb), 0:S5_LANES]
        nxi = ar * xi + ai * xr + x_ref[pl.ds(r0, nb), S5_LANES:2 * S5_LANES]
        x_ref[pl.ds(r0, nb), 0:S5_LANES] = nxr
        x_ref[pl.ds(r0, nb), S5_LANES:2 * S5_LANES] = nxi
        return nxr, nxi

    xr, xi = lax.fori_loop(0, tt, step, (st_ref[0:nb, :], st_ref[nb:2 * nb, :]), unroll=4)
    st_ref[0:nb, :] = xr
    st_ref[nb:2 * nb, :] = xi

    y = jnp.dot(x_ref[...].astype(BF16), wc_ref[...], preferred_element_type=F32)
    y = _gelu_tanh(y + d_ref[...] * u)
    gate = jnp.dot(y.astype(BF16), gw_ref[...], preferred_element_type=F32) + gb_ref[...]
    o_ref[...] = (y * jax.nn.sigmoid(gate)).astype(o_ref.dtype)


def _s5(u_t, wb, ar, ai, wc, d, gw, gb, nb, tt):
    n = u_t.shape[0]
    rows = nb * tt
    kernel = functools.partial(_s5_kernel, tt=tt, nb=nb)
    return pl.pallas_call(
        kernel,
        grid=(n // rows,),
        in_specs=[pl.BlockSpec((rows, S5_WIDTH), lambda i: (i, 0)),
                  _const_spec(wb.shape), _const_spec(ar.shape), _const_spec(ai.shape),
                  _const_spec(wc.shape), _const_spec(d.shape), _const_spec(gw.shape),
                  _const_spec(gb.shape)],
        out_specs=pl.BlockSpec((rows, S5_WIDTH), lambda i: (i, 0)),
        out_shape=jax.ShapeDtypeStruct((n, S5_WIDTH), BF16),
        scratch_shapes=[pltpu.VMEM((rows, 2 * S5_LANES), F32),
                        pltpu.VMEM((2 * nb, S5_LANES), F32)],
        compiler_params=_cparams("arbitrary"),
        name="s5_mixer",
    )(u_t, wb, ar, ai, wc, d, gw, gb)


_NT = (((1,), (1,)), ((), ()))
_TN = (((0,), (0,)), ((), ()))
_HI = lax.Precision.HIGHEST


def _ssd_kernel(z_ref, cw_ref, cb_ref, dtb_ref, a_ref, dsk_ref, ng_ref, tri_ref, trit_ref,
                o_ref, ext_ref, st_ref, *, cl):
    @pl.when(pl.program_id(1) == 0)
    def _():
        st_ref[...] = jnp.zeros(st_ref.shape, F32)
        ext_ref[0:SUBLANES, :] = jnp.zeros((SUBLANES, SSD_XBC), F32)

    tile = z_ref[0]
    zg = tile[:, 0:SSD_WIDTH]
    ext_ref[SUBLANES:SUBLANES + cl, :] = tile[:, SSD_WIDTH:SSD_WIDTH + SSD_XBC]
    conv = cb_ref[...]
    for k in range(SSD_CONV):
        start = SUBLANES - (SSD_CONV - 1) + k
        conv = conv + cw_ref[k:k + 1, :] * ext_ref[start:start + cl, :]
    ext_ref[0:SUBLANES, :] = ext_ref[cl:cl + SUBLANES, :]
    xa = conv * jax.nn.sigmoid(conv)
    xs = xa[:, 0:SSD_WIDTH]
    bm = xa[:, SSD_WIDTH:SSD_WIDTH + SSD_NGROUPS * SSD_STATE]
    cm = xa[:, SSD_WIDTH + SSD_NGROUPS * SSD_STATE:SSD_XBC]

    dtx = tile[:, SSD_WIDTH + SSD_XBC:SSD_SLAB] + dtb_ref[...]
    dt = jnp.maximum(dtx, 0.0) + jnp.log1p(jnp.exp(-jnp.abs(dtx)))
    a_dt = dt * a_ref[...]
    acs = jnp.dot(tri_ref[...], a_dt, precision=_HI, preferred_element_type=F32)
    acs_row = jnp.dot(a_dt.T[0:SUBLANES], trit_ref[...], precision=_HI, preferred_element_type=F32)
    causal = (lax.broadcasted_iota(jnp.int32, (cl, cl), 0)
              >= lax.broadcasted_iota(jnp.int32, (cl, cl), 1))

    cb_scores = []
    for g in range(SSD_NGROUPS):
        bg = bm[:, g * SSD_STATE:(g + 1) * SSD_STATE].astype(BF16)
        cg = cm[:, g * SSD_STATE:(g + 1) * SSD_STATE].astype(BF16)
        cb_scores.append(lax.dot_general(cg, bg, _NT, preferred_element_type=F32))

    rep = SSD_HEADS // SSD_NGROUPS
    ys = []
    for h in range(SSD_HEADS):
        g = h // rep
        acol = acs[:, h:h + 1]
        arow = acs_row[h:h + 1, :]
        alast = acs[cl - 1:cl, h:h + 1]
        decay = jnp.exp(jnp.where(causal, acol - arow, -jnp.inf))
        xs_h = xs[:, h * SSD_HEAD_DIM:(h + 1) * SSD_HEAD_DIM]
        xdt = (xs_h * dt[:, h:h + 1]).astype(BF16)
        y = jnp.dot((cb_scores[g] * decay).astype(BF16), xdt, preferred_element_type=F32)
        bg = bm[:, g * SSD_STATE:(g + 1) * SSD_STATE]
        cg = cm[:, g * SSD_STATE:(g + 1) * SSD_STATE].astype(BF16)
        prev = st_ref[h]
        y = y + jnp.dot(cg, prev.astype(BF16), preferred_element_type=F32) * jnp.exp(acol)
        bdec = (bg * jnp.exp(alast - acol)).astype(BF16)
        st_ref[h] = prev * jnp.exp(alast) + lax.dot_general(bdec, xdt, _TN, preferred_element_type=F32)
        ys.append(y + dsk_ref[:, h:h + 1] * xs_h)

    y = jnp.concatenate(ys, axis=-1) * (zg * jax.nn.sigmoid(zg))
    gw = SSD_WIDTH // SSD_NGROUPS
    outs = []
    for g in range(SSD_NGROUPS):
        yg = y[:, g * gw:(g + 1) * gw]
        outs.append(yg * lax.rsqrt(jnp.mean(yg * yg, axis=-1, keepdims=True) + EPS))
    o_ref[0] = (jnp.concatenate(outs, axis=-1) * ng_ref[...]).astype(o_ref.dtype)


def _ssd(z3, cw, cb, dtb, a, dsk, ng, tri, trit, cl):
    b, s, _ = z3.shape
    kernel = functools.partial(_ssd_kernel, cl=cl)
    return pl.pallas_call(
        kernel,
        grid=(b, s // cl),
        in_specs=[pl.BlockSpec((1, cl, SSD_SLAB), lambda bi, j: (bi, j, 0)),
                  _const_spec(cw.shape), _const_spec(cb.shape), _const_spec(dtb.shape),
                  _const_spec(a.shape), _const_spec(dsk.shape), _const_spec(ng.shape),
                  _const_spec(tri.shape), _const_spec(trit.shape)],
        out_specs=pl.BlockSpec((1, cl, SSD_WIDTH), lambda bi, j: (bi, j, 0)),
        out_shape=jax.ShapeDtypeStruct((b, s, SSD_WIDTH), BF16),
        scratch_shapes=[pltpu.VMEM((cl + 2 * SUBLANES, SSD_XBC), F32),
                        pltpu.VMEM((SSD_HEADS, SSD_STATE, SSD_HEAD_DIM), F32)],
        compiler_params=_cparams("parallel", "arbitrary"),
        name="ssd_mixer",
    )(z3, cw, cb, dtb, a, dsk, ng, tri, trit)


def _out_mlp_kernel(x_ref, a_ref, s_ref, m_ref, wo_ref, g_ref, wu_ref, wd_ref, o_ref, *, fc):
    wo = wo_ref
    mix = jnp.dot(a_ref[...], wo[0:ATTN_WIDTH, :], preferred_element_type=F32)
    mix = mix + jnp.dot(s_ref[...], wo[ATTN_WIDTH:ATTN_WIDTH + S5_WIDTH, :], preferred_element_type=F32)
    mix = mix + jnp.dot(m_ref[...], wo[ATTN_WIDTH + S5_WIDTH:, :], preferred_element_type=F32)
    x1 = x_ref[...] + mix
    ms = jnp.mean(x1 * x1, axis=-1, keepdims=True)
    h = (x1 * lax.rsqrt(ms + EPS) * g_ref[...]).astype(BF16)
    acc = x1
    for c in range(wu_ref.shape[1] // fc):
        up = jnp.maximum(jnp.dot(h, wu_ref[:, c * fc:(c + 1) * fc], preferred_element_type=F32), 0.0)
        acc = acc + jnp.dot((up * up).astype(BF16), wd_ref[c * fc:(c + 1) * fc, :],
                            preferred_element_type=F32)
    o_ref[...] = acc


def _out_mlp(x2d, attn, s5o, ssdo, wo, g, wu, wd, tm, fc):
    n, d = x2d.shape

    def row_spec(width):
        return pl.BlockSpec((tm, width), lambda i: (i, 0))

    def resident(shape):
        return pl.BlockSpec(shape, lambda i: (0, 0), pipeline_mode=pl.Buffered(1))

    return pl.pallas_call(
        functools.partial(_out_mlp_kernel, fc=fc),
        grid=(n // tm,),
        in_specs=[row_spec(d), row_spec(ATTN_WIDTH), row_spec(S5_WIDTH), row_spec(SSD_WIDTH),
                  resident(wo.shape), resident(g.shape), resident(wu.shape), resident(wd.shape)],
        out_specs=row_spec(d),
        out_shape=jax.ShapeDtypeStruct((n, d), F32),
        compiler_params=_cparams("parallel"),
        name="out_proj_mlp",
    )(x2d, attn, s5o, ssdo, wo, g, wu, wd)


def _pad_w_in(w_in_l):
    d = w_in_l.shape[0]
    sizes = (ATTN_WIDTH, KV_WIDTH, KV_WIDTH, IDX_HEADS * IDX_DIM, IDX_DIM, IDX_HEADS,
             S5_WIDTH, SSD_WIDTH, SSD_XBC, SSD_HEADS)
    offs = np.concatenate([[0], np.cumsum(sizes)])
    dst = (OFF_Q, OFF_K, OFF_V, OFF_QI, OFF_KW, OFF_KW + IDX_DIM,
           OFF_S5, OFF_SSD, OFF_SSD + SSD_WIDTH, OFF_SSD + SSD_WIDTH + SSD_XBC)
    pieces, col = [], 0
    for i in np.argsort(dst):
        pieces.append(jnp.zeros((d, dst[i] - col), w_in_l.dtype))
        pieces.append(w_in_l[:, offs[i]:offs[i] + sizes[i]])
        col = dst[i] + sizes[i]
    pieces.append(jnp.zeros((d, N_IN_PAD - col), w_in_l.dtype))
    return jnp.concatenate(pieces, axis=1).astype(BF16)


def _rope_tables(s):
    pos = jnp.arange(s, dtype=F32)
    inv_freq = ROPE_THETA ** (-jnp.arange(0, ROPE_DIM, 2, dtype=F32) / ROPE_DIM)
    ang = pos[:, None] * inv_freq[None, :]
    cos, sin = jnp.cos(ang), jnp.sin(ang)
    zeros = jnp.zeros((s, ROPE_HALF), F32)
    rest = HEAD_DIM - ROPE_DIM
    c = jnp.concatenate([cos, cos, jnp.ones((s, rest), F32)], axis=-1)
    sa = jnp.concatenate([-sin, zeros, jnp.zeros((s, rest), F32)], axis=-1)
    sb = jnp.concatenate([zeros, sin, jnp.zeros((s, rest), F32)], axis=-1)
    return c, sa, sb


def _block_diag_mean(width, seg, active=None):
    idx = np.arange(width)
    m = (idx[:, None] // seg == idx[None, :] // seg).astype(np.float32) / seg
    if active is not None:
        m = m * (idx[:, None] < active) * (idx[None, :] < active)
    return jnp.asarray(m, BF16)


def _pad_lanes(v, fill=0.0):
    v = v.reshape(1, -1).astype(F32)
    return jnp.pad(v, ((0, 0), (0, LANES - v.shape[1])), constant_values=fill)


def _s5_params(lam_re, lam_im, log_step, b_re, b_im, c_re, c_im):
    step = jnp.exp(log_step.astype(F32))[:, None]
    lr, li = lam_re.astype(F32), lam_im.astype(F32)
    mag = jnp.exp(lr * step)
    ab_re = mag * jnp.cos(li * step)
    ab_im = mag * jnp.sin(li * step)
    den = lr * lr + li * li
    cr = ((ab_re - 1.0) * lr + ab_im * li) / den
    ci = (ab_im * lr - (ab_re - 1.0) * li) / den
    bb_re = cr[..., None] * b_re - ci[..., None] * b_im
    bb_im = cr[..., None] * b_im + ci[..., None] * b_re
    eye = jnp.eye(S5_GROUPS, dtype=F32)

    def in_bd(bb):
        return jnp.einsum('gph,gk->ghkp', bb, eye).reshape(S5_WIDTH, S5_LANES)

    def out_bd(cc):
        return jnp.einsum('ghp,gk->gpkh', cc, eye).reshape(S5_LANES, S5_WIDTH)

    wb = jnp.concatenate([in_bd(bb_re), in_bd(bb_im)], axis=1).astype(BF16)
    wc = jnp.concatenate([out_bd(c_re.astype(F32)), -out_bd(c_im.astype(F32))], axis=0).astype(BF16)
    return wb, ab_re.reshape(1, S5_LANES), ab_im.reshape(1, S5_LANES), wc


def _tile(n, pref):
    return pref if n % pref == 0 else n


def kernel(x, norm_mix_g, w_in, attn_q_norm_g, attn_k_norm_g, idx_k_norm_g, s5_lambda_re, s5_lambda_im, s5_log_step, s5_b_re, s5_b_im, s5_c_re, s5_c_im, s5_d, s5_glu_w, s5_glu_b, ssd_conv_w, ssd_conv_b, ssd_dt_bias, ssd_a_log, ssd_d, ssd_norm_g, w_out, norm_mlp_g, w_up, w_down):
    b, s, d = x.shape
    n = b * s
    depth = w_in.shape[0]
    assert b == SUBLANES, "the S5 scan keeps one sequence per sublane"
    tm = _tile(n, 512)
    tt = _tile(s, 512)
    tq = _tile(s, 512)
    ck = _tile(s, 512)
    cl = _tile(s, 256)
    ts5 = _tile(s, 64)

    tables = _rope_tables(s)
    bdq = _block_diag_mean(ATTN_WIDTH, HEAD_DIM)
    bdk = _block_diag_mean(KV_WIDTH, HEAD_DIM)
    bdi = _block_diag_mean(LANES, IDX_DIM, active=IDX_DIM)
    tri = jnp.asarray(np.tril(np.ones((cl, cl), np.float32)))
    trit = jnp.asarray(np.triu(np.ones((cl, cl), np.float32)))

    x2d = x.reshape(n, d)
    for l in range(depth):
        qg = jnp.tile(attn_q_norm_g[l].astype(F32), ATTN_HEADS).reshape(1, ATTN_WIDTH)
        kg = jnp.tile(attn_k_norm_g[l].astype(F32), ATTN_KV_HEADS).reshape(1, KV_WIDTH)
        kig = _pad_lanes(idx_k_norm_g[l])
        z3, qt, kh, vt, qit, kir, wt = _in_proj(
            x2d.reshape(b, s, d), norm_mix_g[l].reshape(1, d).astype(F32), _pad_w_in(w_in[l]),
            tables, qg, kg, kig, bdq, bdk, bdi, tt, tq)
        attn = _dsa(qt, kh, vt, qit, kir, wt, tq, ck)

        wb, ar, ai, wc = _s5_params(s5_lambda_re[l], s5_lambda_im[l], s5_log_step[l],
                                    s5_b_re[l], s5_b_im[l], s5_c_re[l], s5_c_im[l])
        u_t = z3[:, :, OFF_S5 - ATTN_SLAB:].transpose(1, 0, 2).reshape(n, S5_WIDTH)
        s5_t = _s5(u_t, wb, ar, ai, wc, s5_d[l].reshape(1, S5_WIDTH).astype(F32),
                   s5_glu_w[l].astype(BF16), s5_glu_b[l].reshape(1, S5_WIDTH).astype(F32), b, ts5)
        s5o = s5_t.reshape(s, b, S5_WIDTH).transpose(1, 0, 2).reshape(n, S5_WIDTH)

        ssdo = _ssd(z3, ssd_conv_w[l].astype(F32), ssd_conv_b[l].reshape(1, SSD_XBC).astype(F32),
                    _pad_lanes(ssd_dt_bias[l]), _pad_lanes(-jnp.exp(ssd_a_log[l].astype(F32))),
                    _pad_lanes(ssd_d[l]), ssd_norm_g[l].reshape(1, SSD_WIDTH).astype(F32),
                    tri, trit, cl)

        x2d = _out_mlp(x2d, attn.reshape(n, ATTN_WIDTH), s5o, ssdo.reshape(n, SSD_WIDTH),
                       w_out[l].astype(BF16), norm_mlp_g[l].reshape(1, d).astype(F32),
                       w_up[l].astype(BF16), w_down[l].astype(BF16), tm, 1024)
    return x2d.reshape(b, s, d)
```

```python
import functools
import math

import jax
import jax.numpy as jnp
import numpy as np
from jax import lax
from jax.experimental import pallas as pl
from jax.experimental.pallas import tpu as pltpu

F32 = jnp.float32
BF16 = jnp.bfloat16

HEAD_DIM = 64
ATTN_HEADS = 8
ATTN_KV_HEADS = 2
ATTN_REP = ATTN_HEADS // ATTN_KV_HEADS
ATTN_WIDTH = ATTN_HEADS * HEAD_DIM
KV_WIDTH = ATTN_KV_HEADS * HEAD_DIM
ATTN_SCALE = HEAD_DIM ** -0.5
ROPE_DIM = HEAD_DIM // 4
ROPE_HALF = ROPE_DIM // 2
ROPE_THETA = 500000.0
IDX_HEADS = 8
IDX_DIM = HEAD_DIM
IDX_W_SCALE = (IDX_HEADS ** -0.5) * (IDX_DIM ** -0.5)
TOPK_MAX = 256
S5_WIDTH = 256
S5_GROUP_CH = 16
S5_GROUPS = S5_WIDTH // S5_GROUP_CH
S5_STATE = 64
S5_LANES = S5_GROUPS * S5_STATE
SSD_WIDTH = 256
SSD_HEAD_DIM = 64
SSD_HEADS = SSD_WIDTH // SSD_HEAD_DIM
SSD_NGROUPS = 2
SSD_STATE = 64
SSD_CONV = 4
SSD_XBC = SSD_WIDTH + 2 * SSD_NGROUPS * SSD_STATE
EPS = 1e-6
LANES = 128
SUBLANES = 8

ATTN_SLAB = 1536
OFF_Q, OFF_K, OFF_V, OFF_QI, OFF_KW = 0, 512, 640, 768, 1280
SSD_SLAB = 896
OFF_SSD = ATTN_SLAB
OFF_S5 = OFF_SSD + SSD_SLAB
N_IN_PAD = OFF_S5 + S5_WIDTH
REST_WIDTH = N_IN_PAD - ATTN_SLAB

INT32_MIN = -2 ** 31
INT32_MAX = 2 ** 31 - 1
ZERO_BAND_LO = -2 ** 23
LOG2E = math.log2(math.e)
NEG_BIG = -1e30
COUNT_CHAINS = 4
ATTN_SUB_KEYS = 512
HEADS_PER_DOT = 4
VMEM_LIMIT = 56 * 1024 * 1024


def _cparams(*sem):
    return pltpu.CompilerParams(dimension_semantics=sem, vmem_limit_bytes=VMEM_LIMIT)


def _const_spec(shape):
    nd = len(shape)
    return pl.BlockSpec(shape, lambda *_: (0,) * nd)


def _segment_mean(sq, bd_ref):
    hi = sq.astype(BF16)
    lo = (sq - hi.astype(F32)).astype(BF16)
    bd = bd_ref[...]
    return (jnp.dot(hi, bd, preferred_element_type=F32)
            + jnp.dot(lo, bd, preferred_element_type=F32))


def _rope(x, c, sa, sb):
    w = x.shape[-1]
    reps = w // HEAD_DIM
    c, sa, sb = (jnp.tile(t, (1, reps)) if reps > 1 else t for t in (c, sa, sb))
    return x * c + pltpu.roll(x, w - ROPE_HALF, 1) * sa + pltpu.roll(x, ROPE_HALF, 1) * sb


def _in_proj_kernel(x_ref, g_ref, w_ref, c_ref, sa_ref, sb_ref, qg_ref, kg_ref, kig_ref,
                    bdq_ref, bdk_ref, bdi_ref,
                    rest_out, qt_out, k_out, vt_out, qit_out, ki_out, wt_out, *, tq):
    x = x_ref[...]
    ms = jnp.mean(x * x, axis=-1, keepdims=True)
    h = (x * lax.rsqrt(ms + EPS) * g_ref[...]).astype(BF16)
    z = jnp.dot(h, w_ref[...], preferred_element_type=F32)
    rest_out[...] = z[:, ATTN_SLAB:]
    tt = z.shape[0]
    c, sa, sb = c_ref[...], sa_ref[...], sb_ref[...]
    q = z[:, OFF_Q:OFF_Q + ATTN_WIDTH]
    qn = q * lax.rsqrt(_segment_mean(q * q, bdq_ref) + EPS) * qg_ref[...]
    qn = _rope(qn, c, sa, sb) * (ATTN_SCALE * LOG2E)
    k = z[:, OFF_K:OFF_K + KV_WIDTH]
    kn = k * lax.rsqrt(_segment_mean(k * k, bdk_ref) + EPS) * kg_ref[...]
    kn = _rope(kn, c, sa, sb)
    for h in range(ATTN_KV_HEADS):
        k_out[0, h] = kn[:, h * HEAD_DIM:(h + 1) * HEAD_DIM].astype(BF16)
    v = z[:, OFF_V:OFF_V + KV_WIDTH]
    qi = _rope(z[:, OFF_QI:OFF_QI + IDX_HEADS * IDX_DIM], c, sa, sb)
    kw = z[:, OFF_KW:OFF_KW + LANES]
    kin = kw * lax.rsqrt(_segment_mean(kw * kw, bdi_ref) + EPS) * kig_ref[...]
    kin = _rope(kin, c, sa, sb)
    ki_out[0] = kin[:, 0:IDX_DIM].astype(BF16)

    for i in range(tt // tq):
        rows = slice(i * tq, (i + 1) * tq)
        qt = qn[rows].T
        qit = qi[rows].T
        for h in range(ATTN_HEADS):
            g, r = divmod(h, ATTN_REP)
            qt_out[0, i, g, :, r * tq:(r + 1) * tq] = qt[h * HEAD_DIM:(h + 1) * HEAD_DIM].astype(BF16)
            qit_out[0, i, h] = qit[h * IDX_DIM:(h + 1) * IDX_DIM].astype(BF16)
        wt_out[0, i] = kw[rows].T[IDX_DIM:IDX_DIM + IDX_HEADS] * IDX_W_SCALE
    for i in range(tt // LANES):
        vt = v[i * LANES:(i + 1) * LANES].T
        for g in range(ATTN_KV_HEADS):
            vt_out[0, g, i] = vt[g * HEAD_DIM:(g + 1) * HEAD_DIM].astype(BF16)


def _in_proj(x3, g, w, tables, qg, kg, kig, bdq, bdk, bdi, tt, tq):
    b, s, d = x3.shape
    nq, nsub, nv = s // tq, tt // tq, tt // LANES
    tab_spec = pl.BlockSpec((tt, HEAD_DIM), lambda bi, j: (j, 0))
    return pl.pallas_call(
        functools.partial(_in_proj_kernel, tq=tq),
        grid=(b, s // tt),
        in_specs=[pl.BlockSpec((None, tt, d), lambda bi, j: (bi, j, 0)),
                  _const_spec(g.shape),
                  pl.BlockSpec(w.shape, lambda bi, j: (0, 0), pipeline_mode=pl.Buffered(1)),
                  tab_spec, tab_spec, tab_spec,
                  _const_spec(qg.shape), _const_spec(kg.shape), _const_spec(kig.shape),
                  _const_spec(bdq.shape), _const_spec(bdk.shape), _const_spec(bdi.shape)],
        out_specs=[pl.BlockSpec((None, tt, REST_WIDTH), lambda bi, j: (bi, j, 0)),
                   pl.BlockSpec((1, nsub, ATTN_KV_HEADS, HEAD_DIM, ATTN_REP * tq),
                                lambda bi, j: (bi, j, 0, 0, 0)),
                   pl.BlockSpec((1, ATTN_KV_HEADS, tt, HEAD_DIM), lambda bi, j: (bi, 0, j, 0)),
                   pl.BlockSpec((1, ATTN_KV_HEADS, nv, HEAD_DIM, LANES), lambda bi, j: (bi, 0, j, 0, 0)),
                   pl.BlockSpec((1, nsub, IDX_HEADS, IDX_DIM, tq), lambda bi, j: (bi, j, 0, 0, 0)),
                   pl.BlockSpec((1, tt, IDX_DIM), lambda bi, j: (bi, j, 0)),
                   pl.BlockSpec((1, nsub, IDX_HEADS, tq), lambda bi, j: (bi, j, 0, 0))],
        out_shape=[jax.ShapeDtypeStruct((b, s, REST_WIDTH), F32),
                   jax.ShapeDtypeStruct((b, nq, ATTN_KV_HEADS, HEAD_DIM, ATTN_REP * tq), BF16),
                   jax.ShapeDtypeStruct((b, ATTN_KV_HEADS, s, HEAD_DIM), BF16),
                   jax.ShapeDtypeStruct((b, ATTN_KV_HEADS, s // LANES, HEAD_DIM, LANES), BF16),
                   jax.ShapeDtypeStruct((b, nq, IDX_HEADS, IDX_DIM, tq), BF16),
                   jax.ShapeDtypeStruct((b, s, IDX_DIM), BF16),
                   jax.ShapeDtypeStruct((b, nq, IDX_HEADS, tq), F32)],
        compiler_params=_cparams("parallel", "parallel"),
        name="in_proj_prep",
    )(x3, g, w, *tables, qg, kg, kig, bdq, bdk, bdi)


def _order_key(x):
    return x ^ (lax.shift_right_arithmetic(x, 31) & 0x7FFFFFFF)


def _f32_to_key(v):
    return _order_key(lax.bitcast_convert_type(v, jnp.int32))


def _key_to_f32(key):
    return lax.bitcast_convert_type(_order_key(key), F32)


def _dsa_kernel(qi_ref, w_ref, q_ref, ki_ref, k_ref, v_ref, o_ref,
                sc_ref, thr_ref, m_ref, acc_ref, *, tq, ck, topk, interp_rounds):
    qb = pl.program_id(1)
    n_chunks = lax.div(qb * tq, jnp.int32(ck)) + 1
    nr = ck // SUBLANES
    shape3 = (nr, SUBLANES, tq)
    q_pos = qb * tq + lax.broadcasted_iota(jnp.int32, (1, tq), 1)
    q_pos_b = jnp.broadcast_to(q_pos, (SUBLANES, tq))[None]
    key_id = (lax.broadcasted_iota(jnp.int32, shape3, 0) * SUBLANES
              + lax.broadcasted_iota(jnp.int32, shape3, 1))

    w = w_ref[0, 0]
    wb = [jnp.broadcast_to(w[h:h + 1, :], (SUBLANES, tq))[None] for h in range(IDX_HEADS)]

    def score_chunk(c, carry, diag):
        smax, smin = carry
        off = pl.multiple_of(c * ck, ck)
        kic = ki_ref[0, pl.ds(off, ck), :]
        sc = None
        for h in range(IDX_HEADS):
            lg = jnp.dot(kic, qi_ref[0, 0, h], preferred_element_type=F32)
            t = jnp.maximum(lg.reshape(shape3), 0.0) * wb[h]
            sc = t if sc is None else sc + t
        kid = key_id + off
        key = jnp.where(sc == 0.0, -2 - kid, _f32_to_key(sc))
        if diag:
            adm = kid <= q_pos_b
            smax = jnp.maximum(smax, jnp.max(jnp.where(adm, sc, -jnp.inf), axis=0))
            smin = jnp.minimum(smin, jnp.min(jnp.where(adm, sc, jnp.inf), axis=0))
            key = jnp.where(adm, key, INT32_MIN)
        else:
            smax = jnp.maximum(smax, jnp.max(sc, axis=0))
            smin = jnp.minimum(smin, jnp.min(sc, axis=0))
        sc_ref[c] = key.reshape(ck, tq)
        return smax, smin

    ext = (jnp.full((SUBLANES, tq), -jnp.inf, F32), jnp.full((SUBLANES, tq), jnp.inf, F32))
    ext = lax.fori_loop(0, n_chunks - 1, functools.partial(score_chunk, diag=False), ext)
    smax, smin = score_chunk(n_chunks - 1, ext, True)
    kmax = _f32_to_key(jnp.max(smax, axis=0, keepdims=True))
    kmin = _f32_to_key(jnp.min(smin, axis=0, keepdims=True))

    def count(pred):
        def body(c, acc):
            hits = pred(sc_ref[c].reshape(shape3), c).astype(jnp.int32)
            return acc + jnp.sum(hits.reshape(nr // COUNT_CHAINS, COUNT_CHAINS, SUBLANES, tq), axis=0)
        acc = lax.fori_loop(0, n_chunks, body, jnp.zeros((COUNT_CHAINS, SUBLANES, tq), jnp.int32))
        return jnp.sum(jnp.sum(acc, axis=0), axis=0, keepdims=True)

    def count_ge(kx):
        kx_b = jnp.broadcast_to(kx, (SUBLANES, tq))[None]
        return count(lambda x, c: x >= kx_b)

    n_adm = q_pos + 1
    log_k = math.log(topk - 0.5)
    c_pos = count_ge(jnp.zeros((1, tq), jnp.int32))
    pos = c_pos > topk
    lg_pos = jnp.log(jnp.maximum(c_pos, 1).astype(F32))
    frac1 = jnp.clip((lg_pos - log_k) / (lg_pos - math.log(0.5)), 0.0, 1.0)
    k1_pos = jnp.minimum(jnp.maximum(_f32_to_key(_key_to_f32(kmax + 1) * frac1), 1), kmax)
    k1 = jnp.where(pos, k1_pos, ZERO_BAND_LO)
    c1 = count_ge(k1)
    c_nn = c1
    band = jnp.logical_and(jnp.logical_and(c_pos < topk, c_nn > topk), jnp.logical_not(pos))
    all_in = n_adm <= topk
    done0 = jnp.logical_or(all_in, jnp.logical_or(c_pos == topk, c1 == topk))
    thr0 = jnp.where(all_in, INT32_MIN + 1, jnp.where(c_pos == topk, 0, k1))
    up1 = jnp.logical_and(pos, c1 > topk)
    down1 = jnp.logical_and(pos, c1 < topk)
    lo0 = jnp.where(band, -2 - n_chunks * ck, jnp.where(pos, jnp.where(up1, k1, 0), kmin))
    hi0 = jnp.where(band, -1, jnp.where(pos, jnp.where(down1, k1, kmax + 1), ZERO_BAND_LO))
    clo0 = jnp.where(band, c_nn, jnp.where(pos, jnp.where(up1, c1, c_pos), n_adm)).astype(F32)
    chi0 = jnp.where(band, c_pos, jnp.where(pos, jnp.where(down1, c1, 0), c_nn)).astype(F32)

    def as_value(key):
        return jnp.where(band, key.astype(F32), _key_to_f32(key))

    def search_step(st, how):
        lo, hi, c_lo, c_hi, done, thr = st
        span = hi - lo
        if how == "key_mid":
            kx = lo + lax.shift_right_logical(span, 1)
        else:
            if how == "value_mid":
                frac = 0.5
            else:
                lin = (c_lo - (topk - 0.5)) / (c_lo - c_hi)
                lg_lo = jnp.log(c_lo)
                lg = (lg_lo - log_k) / (lg_lo - jnp.log(jnp.maximum(c_hi, 0.5)))
                frac = jnp.clip(jnp.where(band, lin, lg), 0.0, 1.0)
            lo_v, hi_v = as_value(lo), as_value(hi)
            x = lo_v + (hi_v - lo_v) * frac
            kx = jnp.where(band, jnp.floor(x).astype(jnp.int32), _f32_to_key(x))
        kx = jnp.minimum(jnp.maximum(kx, lo + 1), hi - 1)
        cnt = count_ge(kx)
        live = jnp.logical_and(done == 0, span >= 2)
        hit = jnp.logical_and(live, cnt == topk)
        up = jnp.logical_and(live, cnt > topk)
        down = jnp.logical_and(live, cnt < topk)
        cf = cnt.astype(F32)
        return (jnp.where(up, kx, lo), jnp.where(down, kx, hi),
                jnp.where(up, cf, c_lo), jnp.where(down, cf, c_hi),
                jnp.where(hit, 1, done), jnp.where(hit, kx, thr))

    def unsettled(st):
        lo, hi, _, _, done, _ = st
        return jnp.max(jnp.where(jnp.logical_and(done == 0, hi - lo >= 2), 1, 0)) > 0

    def interp_cond(st):
        return jnp.logical_and(st[0] < interp_rounds, unsettled(st[1]))

    def interp_round(st):
        inner = search_step(search_step(search_step(st[1], "interp"), "interp"), "value_mid")
        return st[0] + 1, inner

    st = lax.while_loop(interp_cond, interp_round,
                        (jnp.int32(0), (lo0, hi0, clo0, chi0, done0.astype(jnp.int32), thr0)))[1]
    st = lax.while_loop(unsettled, lambda s: search_step(search_step(s, "key_mid"), "key_mid"), st)
    lo, _, _, c_gt, done, thr = st
    tied = done == 0
    thr = jnp.where(tied, lo, thr)
    thr_ref[...] = thr

    @pl.when(jnp.max(tied.astype(jnp.int32)) > 0)
    def _():
        want = topk - c_gt.astype(jnp.int32)
        thr_b = jnp.broadcast_to(thr, (SUBLANES, tq))[None]

        def bisect_index(_, carry):
            jlo, jhi = carry
            mid = jlo + lax.shift_right_logical(jhi - jlo, 1)
            mid_b = jnp.broadcast_to(mid, (SUBLANES, tq))[None]
            cnt = count(lambda x, c: jnp.logical_and(x == thr_b, key_id + c * ck <= mid_b))
            ge = cnt >= want
            return jnp.where(ge, jlo, mid), jnp.where(ge, mid, jhi)

        n_idx_steps = int(math.ceil(math.log2(sc_ref.shape[0] * ck + 1)))
        jinit = (jnp.full((1, tq), -1, jnp.int32), (n_chunks * ck - 1) + jnp.zeros((1, tq), jnp.int32))
        _, cut = lax.fori_loop(0, n_idx_steps, bisect_index, jinit)
        cut_b = jnp.broadcast_to(jnp.where(tied, cut, INT32_MAX), (SUBLANES, tq))[None]

        def drop(c, carry):
            x = sc_ref[c].reshape(shape3)
            gone = jnp.logical_and(x == thr_b, key_id + c * ck > cut_b)
            sc_ref[c] = jnp.where(gone, INT32_MIN, x).reshape(ck, tq)
            return carry

        lax.fori_loop(0, n_chunks, drop, 0)

    thr_b = jnp.broadcast_to(thr_ref[...], (SUBLANES, tq))[None]
    m_ref[...] = jnp.full(m_ref.shape, NEG_BIG, F32)
    acc_ref[...] = jnp.zeros(acc_ref.shape, F32)
    ak = min(ATTN_SUB_KEYS, ck)
    ones_rows = jnp.ones((acc_ref.shape[1] - HEAD_DIM, ak), BF16)
    vblocks = ck // LANES

    def attend_chunk(c, carry):
        off = pl.multiple_of(c * ck, ck)
        for sub in range(ck // ak):
            r0 = sub * ak
            sel = sc_ref[c, r0:r0 + ak, :].reshape(ak // SUBLANES, SUBLANES, tq) >= thr_b
            bias = jnp.where(sel, 0.0, NEG_BIG).reshape(ak, tq)
            for g in range(ATTN_KV_HEADS):
                kc = k_ref[0, g, pl.ds(off + r0, ak), :]
                vt = jnp.concatenate([v_ref[0, g, c * vblocks + sub * (ak // LANES) + i]
                                      for i in range(ak // LANES)], axis=1)
                vext = jnp.concatenate([vt, ones_rows], axis=0)
                for hp in range(ATTN_REP // HEADS_PER_DOT):
                    l0, l1 = hp * HEADS_PER_DOT * tq, (hp + 1) * HEADS_PER_DOT * tq
                    s = jnp.dot(kc, q_ref[0, 0, g, :, l0:l1], preferred_element_type=F32)
                    ps, alphas = [], []
                    for r in range(HEADS_PER_DOT):
                        h = g * ATTN_REP + hp * HEADS_PER_DOT + r
                        sh = s[:, r * tq:(r + 1) * tq] + bias
                        m_old = m_ref[h]
                        m_new = jnp.maximum(m_old, jnp.max(
                            jnp.max(sh.reshape(ak // SUBLANES, SUBLANES, tq), axis=0), axis=0, keepdims=True))
                        ps.append(jnp.exp2(sh - m_new).astype(BF16))
                        alphas.append(jnp.exp2(m_old - m_new))
                        m_ref[h] = m_new
                    acc_ref[g, :, l0:l1] = (jnp.concatenate(alphas, axis=1) * acc_ref[g, :, l0:l1]
                                            + jnp.dot(vext, jnp.concatenate(ps, axis=1),
                                                      preferred_element_type=F32))
        return carry

    lax.fori_loop(0, n_chunks, attend_chunk, 0)

    outs = []
    for g in range(ATTN_KV_HEADS):
        a = acc_ref[g]
        o = a[0:HEAD_DIM] / a[HEAD_DIM:HEAD_DIM + 1]
        outs += [o[:, r * tq:(r + 1) * tq] for r in range(ATTN_REP)]
    o_ref[0] = jnp.concatenate(outs, axis=0).T.astype(o_ref.dtype)


def _dsa(qt, kh, vt, qit, kir, wt, tq, ck):
    b, _, s, _ = kh.shape
    assert ck % tq == 0, "only the last causal chunk may hold inadmissible keys"
    topk = min(TOPK_MAX, s // 4)
    kernel = functools.partial(_dsa_kernel, tq=tq, ck=ck, topk=topk, interp_rounds=7)
    return pl.pallas_call(
        kernel,
        grid=(b, s // tq),
        in_specs=[pl.BlockSpec((1, 1, IDX_HEADS, IDX_DIM, tq), lambda bi, j: (bi, j, 0, 0, 0)),
                  pl.BlockSpec((1, 1, IDX_HEADS, tq), lambda bi, j: (bi, j, 0, 0)),
                  pl.BlockSpec((1, 1, ATTN_KV_HEADS, HEAD_DIM, ATTN_REP * tq), lambda bi, j: (bi, j, 0, 0, 0)),
                  pl.BlockSpec((1, s, IDX_DIM), lambda bi, j: (bi, 0, 0)),
                  pl.BlockSpec((1, ATTN_KV_HEADS, s, HEAD_DIM), lambda bi, j: (bi, 0, 0, 0)),
                  pl.BlockSpec((1, ATTN_KV_HEADS, s // LANES, HEAD_DIM, LANES),
                               lambda bi, j: (bi, 0, 0, 0, 0))],
        out_specs=pl.BlockSpec((1, tq, ATTN_WIDTH), lambda bi, j: (bi, j, 0)),
        out_shape=jax.ShapeDtypeStruct((b, s, ATTN_WIDTH), BF16),
        scratch_shapes=[pltpu.VMEM((s // ck, ck, tq), jnp.int32),
                        pltpu.VMEM((1, tq), jnp.int32),
                        pltpu.VMEM((ATTN_HEADS, 1, tq), F32),
                        pltpu.VMEM((ATTN_KV_HEADS, HEAD_DIM + 2 * SUBLANES, ATTN_REP * tq), F32)],
        compiler_params=_cparams("parallel", "arbitrary"),
        name="dsa_mixer",
    )(qit, wt, qt, kir, kh, vt)


def _gelu_tanh(y):
    return 0.5 * y * (1.0 + jnp.tanh(math.sqrt(2.0 / math.pi) * (y + 0.044715 * (y * y * y))))


def _s5_kernel(u_ref, wb_ref, ar_ref, ai_ref, wc_ref, d_ref, gw_ref, gb_ref, o_ref,
               x_ref, st_ref, *, tt, nb):
    @pl.when(pl.program_id(0) == 0)
    def _():
        st_ref[...] = jnp.zeros(st_ref.shape, F32)

    u = u_ref[...]
    x_ref[...] = jnp.dot(u.astype(BF16), wb_ref[...], preferred_element_type=F32)
    ar = jnp.broadcast_to(ar_ref[...], (nb, S5_LANES))
    ai = jnp.broadcast_to(ai_ref[...], (nb, S5_LANES))

    def step(t, carry):
        xr, xi = carry
        r0 = pl.multiple_of(t * nb, nb)
        nxr = ar * xr - ai * xi + x_ref[pl.ds(r0, nb), 0:S5_LANES]
        nxi = ar * xi + ai * xr + x_ref[pl.ds(r0, nb), S5_LANES:2 * S5_LANES]
        x_ref[pl.ds(r0, nb), 0:S5_LANES] = nxr
        x_ref[pl.ds(r0, nb), S5_LANES:2 * S5_LANES] = nxi
        return nxr, nxi

    xr, xi = lax.fori_loop(0, tt, step, (st_ref[0:nb, :], st_ref[nb:2 * nb, :]), unroll=4)
    st_ref[0:nb, :] = xr
    st_ref[nb:2 * nb, :] = xi

    y = jnp.dot(x_ref[...].astype(BF16), wc_ref[...], preferred_element_type=F32)
    y = _gelu_tanh(y + d_ref[...] * u)
    gate = jnp.dot(y.astype(BF16), gw_ref[...], preferred_element_type=F32) + gb_ref[...]
    o_ref[...] = (y * jax.nn.sigmoid(gate)).astype(o_ref.dtype)


def _s5(u_t, wb, ar, ai, wc, d, gw, gb, nb, tt):
    n = u_t.shape[0]
    rows = nb * tt
    kernel = functools.partial(_s5_kernel, tt=tt, nb=nb)
    return pl.pallas_call(
        kernel,
        grid=(n // rows,),
        in_specs=[pl.BlockSpec((rows, S5_WIDTH), lambda i: (i, 0)),
                  _const_spec(wb.shape), _const_spec(ar.shape), _const_spec(ai.shape),
                  _const_spec(wc.shape), _const_spec(d.shape), _const_spec(gw.shape),
                  _const_spec(gb.shape)],
        out_specs=pl.BlockSpec((rows, S5_WIDTH), lambda i: (i, 0)),
        out_shape=jax.ShapeDtypeStruct((n, S5_WIDTH), BF16),
        scratch_shapes=[pltpu.VMEM((rows, 2 * S5_LANES), F32),
                        pltpu.VMEM((2 * nb, S5_LANES), F32)],
        compiler_params=_cparams("arbitrary"),
        name="s5_mixer",
    )(u_t, wb, ar, ai, wc, d, gw, gb)


_NT = (((1,), (1,)), ((), ()))
_TN = (((0,), (0,)), ((), ()))
_HI = lax.Precision.HIGHEST


def _ssd_kernel(z_ref, cw_ref, cb_ref, dtb_ref, a_ref, dsk_ref, ng_ref, tri_ref, trit_ref,
                o_ref, ext_ref, st_ref, *, cl):
    @pl.when(pl.program_id(1) == 0)
    def _():
        st_ref[...] = jnp.zeros(st_ref.shape, F32)
        ext_ref[0:SUBLANES, :] = jnp.zeros((SUBLANES, SSD_XBC), F32)

    tile = z_ref[0]
    zg = tile[:, 0:SSD_WIDTH]
    ext_ref[SUBLANES:SUBLANES + cl, :] = tile[:, SSD_WIDTH:SSD_WIDTH + SSD_XBC]
    conv = cb_ref[...]
    for k in range(SSD_CONV):
        start = SUBLANES - (SSD_CONV - 1) + k
        conv = conv + cw_ref[k:k + 1, :] * ext_ref[start:start + cl, :]
    ext_ref[0:SUBLANES, :] = ext_ref[cl:cl + SUBLANES, :]
    xa = conv * jax.nn.sigmoid(conv)
    xs = xa[:, 0:SSD_WIDTH]
    bm = xa[:, SSD_WIDTH:SSD_WIDTH + SSD_NGROUPS * SSD_STATE]
    cm = xa[:, SSD_WIDTH + SSD_NGROUPS * SSD_STATE:SSD_XBC]

    dtx = tile[:, SSD_WIDTH + SSD_XBC:SSD_SLAB] + dtb_ref[...]
    dt = jnp.maximum(dtx, 0.0) + jnp.log1p(jnp.exp(-jnp.abs(dtx)))
    a_dt = dt * a_ref[...]
    acs = jnp.dot(tri_ref[...], a_dt, precision=_HI, preferred_element_type=F32)
    acs_row = jnp.dot(a_dt.T[0:SUBLANES], trit_ref[...], precision=_HI, preferred_element_type=F32)
    causal = (lax.broadcasted_iota(jnp.int32, (cl, cl), 0)
              >= lax.broadcasted_iota(jnp.int32, (cl, cl), 1))

    cb_scores = []
    for g in range(SSD_NGROUPS):
        bg = bm[:, g * SSD_STATE:(g + 1) * SSD_STATE].astype(BF16)
        cg = cm[:, g * SSD_STATE:(g + 1) * SSD_STATE].astype(BF16)
        cb_scores.append(lax.dot_general(cg, bg, _NT, preferred_element_type=F32))

    rep = SSD_HEADS // SSD_NGROUPS
    ys = []
    for h in range(SSD_HEADS):
        g = h // rep
        acol = acs[:, h:h + 1]
        arow = acs_row[h:h + 1, :]
        alast = acs[cl - 1:cl, h:h + 1]
        decay = jnp.exp(jnp.where(causal, acol - arow, -jnp.inf))
        xs_h = xs[:, h * SSD_HEAD_DIM:(h + 1) * SSD_HEAD_DIM]
        xdt = (xs_h * dt[:, h:h + 1]).astype(BF16)
        y = jnp.dot((cb_scores[g] * decay).astype(BF16), xdt, preferred_element_type=F32)
        bg = bm[:, g * SSD_STATE:(g + 1) * SSD_STATE]
        cg = cm[:, g * SSD_STATE:(g + 1) * SSD_STATE].astype(BF16)
        prev = st_ref[h]
        y = y + jnp.dot(cg, prev.astype(BF16), preferred_element_type=F32) * jnp.exp(acol)
        bdec = (bg * jnp.exp(alast - acol)).astype(BF16)
        st_ref[h] = prev * jnp.exp(alast) + lax.dot_general(bdec, xdt, _TN, preferred_element_type=F32)
        ys.append(y + dsk_ref[:, h:h + 1] * xs_h)

    y = jnp.concatenate(ys, axis=-1) * (zg * jax.nn.sigmoid(zg))
    gw = SSD_WIDTH // SSD_NGROUPS
    outs = []
    for g in range(SSD_NGROUPS):
        yg = y[:, g * gw:(g + 1) * gw]
        outs.append(yg * lax.rsqrt(jnp.mean(yg * yg, axis=-1, keepdims=True) + EPS))
    o_ref[0] = (jnp.concatenate(outs, axis=-1) * ng_ref[...]).astype(o_ref.dtype)


def _ssd(z3, cw, cb, dtb, a, dsk, ng, tri, trit, cl):
    b, s, _ = z3.shape
    kernel = functools.partial(_ssd_kernel, cl=cl)
    return pl.pallas_call(
        kernel,
        grid=(b, s // cl),
        in_specs=[pl.BlockSpec((1, cl, SSD_SLAB), lambda bi, j: (bi, j, 0)),
                  _const_spec(cw.shape), _const_spec(cb.shape), _const_spec(dtb.shape),
                  _const_spec(a.shape), _const_spec(dsk.shape), _const_spec(ng.shape),
                  _const_spec(tri.shape), _const_spec(trit.shape)],
        out_specs=pl.BlockSpec((1, cl, SSD_WIDTH), lambda bi, j: (bi, j, 0)),
        out_shape=jax.ShapeDtypeStruct((b, s, SSD_WIDTH), BF16),
        scratch_shapes=[pltpu.VMEM((cl + 2 * SUBLANES, SSD_XBC), F32),
                        pltpu.VMEM((SSD_HEADS, SSD_STATE, SSD_HEAD_DIM), F32)],
        compiler_params=_cparams("parallel", "arbitrary"),
        name="ssd_mixer",
    )(z3, cw, cb, dtb, a, dsk, ng, tri, trit)


def _out_mlp_kernel(x_ref, a_ref, s_ref, m_ref, wo_ref, g_ref, wu_ref, wd_ref, o_ref, *, fc):
    wo = wo_ref
    mix = jnp.dot(a_ref[...], wo[0:ATTN_WIDTH, :], preferred_element_type=F32)
    mix = mix + jnp.dot(s_ref[...], wo[ATTN_WIDTH:ATTN_WIDTH + S5_WIDTH, :], preferred_element_type=F32)
    mix = mix + jnp.dot(m_ref[...], wo[ATTN_WIDTH + S5_WIDTH:, :], preferred_element_type=F32)
    x1 = x_ref[...] + mix
    ms = jnp.mean(x1 * x1, axis=-1, keepdims=True)
    h = (x1 * lax.rsqrt(ms + EPS) * g_ref[...]).astype(BF16)
    acc = x1
    for c in range(wu_ref.shape[1] // fc):
        up = jnp.maximum(jnp.dot(h, wu_ref[:, c * fc:(c + 1) * fc], preferred_element_type=F32), 0.0)
        acc = acc + jnp.dot((up * up).astype(BF16), wd_ref[c * fc:(c + 1) * fc, :],
                            preferred_element_type=F32)
    o_ref[...] = acc


def _out_mlp(x2d, attn, s5o, ssdo, wo, g, wu, wd, tm, fc):
    n, d = x2d.shape

    def row_spec(width):
        return pl.BlockSpec((tm, width), lambda i: (i, 0))

    def resident(shape):
        return pl.BlockSpec(shape, lambda i: (0, 0), pipeline_mode=pl.Buffered(1))

    return pl.pallas_call(
        functools.partial(_out_mlp_kernel, fc=fc),
        grid=(n // tm,),
        in_specs=[row_spec(d), row_spec(ATTN_WIDTH), row_spec(S5_WIDTH), row_spec(SSD_WIDTH),
                  resident(wo.shape), resident(g.shape), resident(wu.shape), resident(wd.shape)],
        out_specs=row_spec(d),
        out_shape=jax.ShapeDtypeStruct((n, d), F32),
        compiler_params=_cparams("parallel"),
        name="out_proj_mlp",
    )(x2d, attn, s5o, ssdo, wo, g, wu, wd)


def _pad_w_in(w_in_l):
    d = w_in_l.shape[0]
    sizes = (ATTN_WIDTH, KV_WIDTH, KV_WIDTH, IDX_HEADS * IDX_DIM, IDX_DIM, IDX_HEADS,
             S5_WIDTH, SSD_WIDTH, SSD_XBC, SSD_HEADS)
    offs = np.concatenate([[0], np.cumsum(sizes)])
    dst = (OFF_Q, OFF_K, OFF_V, OFF_QI, OFF_KW, OFF_KW + IDX_DIM,
           OFF_S5, OFF_SSD, OFF_SSD + SSD_WIDTH, OFF_SSD + SSD_WIDTH + SSD_XBC)
    pieces, col = [], 0
    for i in np.argsort(dst):
        pieces.append(jnp.zeros((d, dst[i] - col), w_in_l.dtype))
        pieces.append(w_in_l[:, offs[i]:offs[i] + sizes[i]])
        col = dst[i] + sizes[i]
    pieces.append(jnp.zeros((d, N_IN_PAD - col), w_in_l.dtype))
    return jnp.concatenate(pieces, axis=1).astype(BF16)


def _rope_tables(s):
    pos = jnp.arange(s, dtype=F32)
    inv_freq = ROPE_THETA ** (-jnp.arange(0, ROPE_DIM, 2, dtype=F32) / ROPE_DIM)
    ang = pos[:, None] * inv_freq[None, :]
    cos, sin = jnp.cos(ang), jnp.sin(ang)
    zeros = jnp.zeros((s, ROPE_HALF), F32)
    rest = HEAD_DIM - ROPE_DIM
    c = jnp.concatenate([cos, cos, jnp.ones((s, rest), F32)], axis=-1)
    sa = jnp.concatenate([-sin, zeros, jnp.zeros((s, rest), F32)], axis=-1)
    sb = jnp.concatenate([zeros, sin, jnp.zeros((s, rest), F32)], axis=-1)
    return c, sa, sb


def _block_diag_mean(width, seg, active=None):
    idx = np.arange(width)
    m = (idx[:, None] // seg == idx[None, :] // seg).astype(np.float32) / seg
    if active is not None:
        m = m * (idx[:, None] < active) * (idx[None, :] < active)
    return jnp.asarray(m, BF16)


def _pad_lanes(v, fill=0.0):
    v = v.reshape(1, -1).astype(F32)
    return jnp.pad(v, ((0, 0), (0, LANES - v.shape[1])), constant_values=fill)


def _s5_params(lam_re, lam_im, log_step, b_re, b_im, c_re, c_im):
    step = jnp.exp(log_step.astype(F32))[:, None]
    lr, li = lam_re.astype(F32), lam_im.astype(F32)
    mag = jnp.exp(lr * step)
    ab_re = mag * jnp.cos(li * step)
    ab_im = mag * jnp.sin(li * step)
    den = lr * lr + li * li
    cr = ((ab_re - 1.0) * lr + ab_im * li) / den
    ci = (ab_im * lr - (ab_re - 1.0) * li) / den
    bb_re = cr[..., None] * b_re - ci[..., None] * b_im
    bb_im = cr[..., None] * b_im + ci[..., None] * b_re
    eye = jnp.eye(S5_GROUPS, dtype=F32)

    def in_bd(bb):
        return jnp.einsum('gph,gk->ghkp', bb, eye).reshape(S5_WIDTH, S5_LANES)

    def out_bd(cc):
        return jnp.einsum('ghp,gk->gpkh', cc, eye).reshape(S5_LANES, S5_WIDTH)

    wb = jnp.concatenate([in_bd(bb_re), in_bd(bb_im)], axis=1).astype(BF16)
    wc = jnp.concatenate([out_bd(c_re.astype(F32)), -out_bd(c_im.astype(F32))], axis=0).astype(BF16)
    return wb, ab_re.reshape(1, S5_LANES), ab_im.reshape(1, S5_LANES), wc


def _tile(n, pref):
    return pref if n % pref == 0 else n


def kernel(x, norm_mix_g, w_in, attn_q_norm_g, attn_k_norm_g, idx_k_norm_g, s5_lambda_re, s5_lambda_im, s5_log_step, s5_b_re, s5_b_im, s5_c_re, s5_c_im, s5_d, s5_glu_w, s5_glu_b, ssd_conv_w, ssd_conv_b, ssd_dt_bias, ssd_a_log, ssd_d, ssd_norm_g, w_out, norm_mlp_g, w_up, w_down):
    b, s, d = x.shape
    n = b * s
    depth = w_in.shape[0]
    assert b == SUBLANES, "the S5 scan keeps one sequence per sublane"
    tm = _tile(n, 512)
    tt = _tile(s, 512)
    tq = _tile(s, 512)
    ck = _tile(s, 512)
    cl = _tile(s, 256)
    ts5 = _tile(s, 64)

    tables = _rope_tables(s)
    bdq = _block_diag_mean(ATTN_WIDTH, HEAD_DIM)
    bdk = _block_diag_mean(KV_WIDTH, HEAD_DIM)
    bdi = _block_diag_mean(LANES, IDX_DIM, active=IDX_DIM)
    tri = jnp.asarray(np.tril(np.ones((cl, cl), np.float32)))
    trit = jnp.asarray(np.triu(np.ones((cl, cl), np.float32)))

    x2d = x.reshape(n, d)
    for l in range(depth):
        qg = jnp.tile(attn_q_norm_g[l].astype(F32), ATTN_HEADS).reshape(1, ATTN_WIDTH)
        kg = jnp.tile(attn_k_norm_g[l].astype(F32), ATTN_KV_HEADS).reshape(1, KV_WIDTH)
        kig = _pad_lanes(idx_k_norm_g[l])
        z3, qt, kh, vt, qit, kir, wt = _in_proj(
            x2d.reshape(b, s, d), norm_mix_g[l].reshape(1, d).astype(F32), _pad_w_in(w_in[l]),
            tables, qg, kg, kig, bdq, bdk, bdi, tt, tq)
        attn = _dsa(qt, kh, vt, qit, kir, wt, tq, ck)

        wb, ar, ai, wc = _s5_params(s5_lambda_re[l], s5_lambda_im[l], s5_log_step[l],
                                    s5_b_re[l], s5_b_im[l], s5_c_re[l], s5_c_im[l])
        u_t = z3[:, :, OFF_S5 - ATTN_SLAB:].transpose(1, 0, 2).reshape(n, S5_WIDTH)
        s5_t = _s5(u_t, wb, ar, ai, wc, s5_d[l].reshape(1, S5_WIDTH).astype(F32),
                   s5_glu_w[l].astype(BF16), s5_glu_b[l].reshape(1, S5_WIDTH).astype(F32), b, ts5)
        s5o = s5_t.reshape(s, b, S5_WIDTH).transpose(1, 0, 2).reshape(n, S5_WIDTH)

        ssdo = _ssd(z3, ssd_conv_w[l].astype(F32), ssd_conv_b[l].reshape(1, SSD_XBC).astype(F32),
                    _pad_lanes(ssd_dt_bias[l]), _pad_lanes(-jnp.exp(ssd_a_log[l].astype(F32))),
                    _pad_lanes(ssd_d[l]), ssd_norm_g[l].reshape(1, SSD_WIDTH).astype(F32),
                    tri, trit, cl)

        x2d = _out_mlp(x2d, attn.reshape(n, ATTN_WIDTH), s5o, ssdo.reshape(n, SSD_WIDTH),
                       w_out[l].astype(BF16), norm_mlp_g[l].reshape(1, d).astype(F32),
                       w_up[l].astype(BF16), w_down[l].astype(BF16), tm, 1024)
    return x2d.reshape(b, s, d)
```

```python
import functools
import math

import jax
import jax.numpy as jnp
import numpy as np
from jax import lax
from jax.experimental import pallas as pl
from jax.experimental.pallas import tpu as pltpu

F32 = jnp.float32
BF16 = jnp.bfloat16

HEAD_DIM = 64
ATTN_HEADS = 8
ATTN_KV_HEADS = 2
ATTN_REP = ATTN_HEADS // ATTN_KV_HEADS
ATTN_WIDTH = ATTN_HEADS * HEAD_DIM
KV_WIDTH = ATTN_KV_HEADS * HEAD_DIM
ATTN_SCALE = HEAD_DIM ** -0.5
ROPE_DIM = HEAD_DIM // 4
ROPE_HALF = ROPE_DIM // 2
ROPE_THETA = 500000.0
IDX_HEADS = 8
IDX_DIM = HEAD_DIM
IDX_W_SCALE = (IDX_HEADS ** -0.5) * (IDX_DIM ** -0.5)
TOPK_MAX = 256
S5_WIDTH = 256
S5_GROUP_CH = 16
S5_GROUPS = S5_WIDTH // S5_GROUP_CH
S5_STATE = 64
S5_LANES = S5_GROUPS * S5_STATE
SSD_WIDTH = 256
SSD_HEAD_DIM = 64
SSD_HEADS = SSD_WIDTH // SSD_HEAD_DIM
SSD_NGROUPS = 2
SSD_STATE = 64
SSD_CONV = 4
SSD_XBC = SSD_WIDTH + 2 * SSD_NGROUPS * SSD_STATE
EPS = 1e-6
LANES = 128
SUBLANES = 8

ATTN_SLAB = 1536
OFF_Q, OFF_K, OFF_V, OFF_QI, OFF_KW = 0, 512, 640, 768, 1280
SSD_SLAB = 896
OFF_SSD = ATTN_SLAB
OFF_S5 = OFF_SSD + SSD_SLAB
N_IN_PAD = OFF_S5 + S5_WIDTH
REST_WIDTH = N_IN_PAD - ATTN_SLAB

INT32_MIN = -2 ** 31
INT32_MAX = 2 ** 31 - 1
ZERO_BAND_LO = -2 ** 23
LOG2E = math.log2(math.e)
NEG_BIG = -1e30
COUNT_CHAINS = 4
SCORE_HEADS_PER_DOT = 4
ATTN_SUB_KEYS = 512
HEADS_PER_DOT = 4
VMEM_LIMIT = 56 * 1024 * 1024


def _cparams(*sem):
    return pltpu.CompilerParams(dimension_semantics=sem, vmem_limit_bytes=VMEM_LIMIT)


def _const_spec(shape):
    nd = len(shape)
    return pl.BlockSpec(shape, lambda *_: (0,) * nd)


def _segment_mean(sq, bd_ref):
    hi = sq.astype(BF16)
    lo = (sq - hi.astype(F32)).astype(BF16)
    bd = bd_ref[...]
    return (jnp.dot(hi, bd, preferred_element_type=F32)
            + jnp.dot(lo, bd, preferred_element_type=F32))


def _rope(x, c, sa, sb):
    w = x.shape[-1]
    reps = w // HEAD_DIM
    c, sa, sb = (jnp.tile(t, (1, reps)) if reps > 1 else t for t in (c, sa, sb))
    return x * c + pltpu.roll(x, w - ROPE_HALF, 1) * sa + pltpu.roll(x, ROPE_HALF, 1) * sb


def _in_proj_kernel(x_ref, g_ref, w_ref, c_ref, sa_ref, sb_ref, qg_ref, kg_ref, kig_ref,
                    bdq_ref, bdk_ref, bdi_ref,
                    rest_out, qt_out, k_out, vt_out, qit_out, ki_out, wt_out, *, tq):
    x = x_ref[...]
    ms = jnp.mean(x * x, axis=-1, keepdims=True)
    h = (x * lax.rsqrt(ms + EPS) * g_ref[...]).astype(BF16)
    z = jnp.dot(h, w_ref[...], preferred_element_type=F32)
    rest_out[...] = z[:, ATTN_SLAB:]
    tt = z.shape[0]
    c, sa, sb = c_ref[...], sa_ref[...], sb_ref[...]
    q = z[:, OFF_Q:OFF_Q + ATTN_WIDTH]
    qn = q * lax.rsqrt(_segment_mean(q * q, bdq_ref) + EPS) * qg_ref[...]
    qn = _rope(qn, c, sa, sb) * (ATTN_SCALE * LOG2E)
    k = z[:, OFF_K:OFF_K + KV_WIDTH]
    kn = k * lax.rsqrt(_segment_mean(k * k, bdk_ref) + EPS) * kg_ref[...]
    kn = _rope(kn, c, sa, sb)
    for h in range(ATTN_KV_HEADS):
        k_out[0, h] = kn[:, h * HEAD_DIM:(h + 1) * HEAD_DIM].astype(BF16)
    v = z[:, OFF_V:OFF_V + KV_WIDTH]
    qi = _rope(z[:, OFF_QI:OFF_QI + IDX_HEADS * IDX_DIM], c, sa, sb)
    kw = z[:, OFF_KW:OFF_KW + LANES]
    kin = kw * lax.rsqrt(_segment_mean(kw * kw, bdi_ref) + EPS) * kig_ref[...]
    kin = _rope(kin, c, sa, sb)
    ki_out[0] = kin[:, 0:IDX_DIM].astype(BF16)

    for i in range(tt // tq):
        rows = slice(i * tq, (i + 1) * tq)
        qt = qn[rows].T
        qit = qi[rows].T
        for h in range(ATTN_HEADS):
            g, r = divmod(h, ATTN_REP)
            qt_out[0, i, g, :, r * tq:(r + 1) * tq] = qt[h * HEAD_DIM:(h + 1) * HEAD_DIM].astype(BF16)
            qit_out[0, i, :, h * tq:(h + 1) * tq] = qit[h * IDX_DIM:(h + 1) * IDX_DIM].astype(BF16)
        wt_out[0, i] = kw[rows].T[IDX_DIM:IDX_DIM + IDX_HEADS] * IDX_W_SCALE
    for i in range(tt // LANES):
        vt = v[i * LANES:(i + 1) * LANES].T
        for g in range(ATTN_KV_HEADS):
            vt_out[0, g, i] = vt[g * HEAD_DIM:(g + 1) * HEAD_DIM].astype(BF16)


def _in_proj(x3, g, w, tables, qg, kg, kig, bdq, bdk, bdi, tt, tq):
    b, s, d = x3.shape
    nq, nsub, nv = s // tq, tt // tq, tt // LANES
    tab_spec = pl.BlockSpec((tt, HEAD_DIM), lambda bi, j: (j, 0))
    return pl.pallas_call(
        functools.partial(_in_proj_kernel, tq=tq),
        grid=(b, s // tt),
        in_specs=[pl.BlockSpec((None, tt, d), lambda bi, j: (bi, j, 0)),
                  _const_spec(g.shape),
                  pl.BlockSpec(w.shape, lambda bi, j: (0, 0), pipeline_mode=pl.Buffered(1)),
                  tab_spec, tab_spec, tab_spec,
                  _const_spec(qg.shape), _const_spec(kg.shape), _const_spec(kig.shape),
                  _const_spec(bdq.shape), _const_spec(bdk.shape), _const_spec(bdi.shape)],
        out_specs=[pl.BlockSpec((None, tt, REST_WIDTH), lambda bi, j: (bi, j, 0)),
                   pl.BlockSpec((1, nsub, ATTN_KV_HEADS, HEAD_DIM, ATTN_REP * tq),
                                lambda bi, j: (bi, j, 0, 0, 0)),
                   pl.BlockSpec((1, ATTN_KV_HEADS, tt, HEAD_DIM), lambda bi, j: (bi, 0, j, 0)),
                   pl.BlockSpec((1, ATTN_KV_HEADS, nv, HEAD_DIM, LANES), lambda bi, j: (bi, 0, j, 0, 0)),
                   pl.BlockSpec((1, nsub, IDX_DIM, IDX_HEADS * tq), lambda bi, j: (bi, j, 0, 0)),
                   pl.BlockSpec((1, tt, IDX_DIM), lambda bi, j: (bi, j, 0)),
                   pl.BlockSpec((1, nsub, IDX_HEADS, tq), lambda bi, j: (bi, j, 0, 0))],
        out_shape=[jax.ShapeDtypeStruct((b, s, REST_WIDTH), F32),
                   jax.ShapeDtypeStruct((b, nq, ATTN_KV_HEADS, HEAD_DIM, ATTN_REP * tq), BF16),
                   jax.ShapeDtypeStruct((b, ATTN_KV_HEADS, s, HEAD_DIM), BF16),
                   jax.ShapeDtypeStruct((b, ATTN_KV_HEADS, s // LANES, HEAD_DIM, LANES), BF16),
                   jax.ShapeDtypeStruct((b, nq, IDX_DIM, IDX_HEADS * tq), BF16),
                   jax.ShapeDtypeStruct((b, s, IDX_DIM), BF16),
                   jax.ShapeDtypeStruct((b, nq, IDX_HEADS, tq), F32)],
        compiler_params=_cparams("parallel", "parallel"),
        name="in_proj_prep",
    )(x3, g, w, *tables, qg, kg, kig, bdq, bdk, bdi)


def _order_key(x):
    return x ^ (lax.shift_right_arithmetic(x, 31) & 0x7FFFFFFF)


def _f32_to_key(v):
    return _order_key(lax.bitcast_convert_type(v, jnp.int32))


def _key_to_f32(key):
    return lax.bitcast_convert_type(_order_key(key), F32)


def _dsa_kernel(qi_ref, w_ref, q_ref, ki_ref, k_ref, v_ref, o_ref,
                sc_ref, thr_ref, m_ref, acc_ref, *, tq, ck, topk, interp_rounds):
    qb = pl.program_id(1)
    n_chunks = lax.div(qb * tq, jnp.int32(ck)) + 1
    nr = ck // SUBLANES
    shape3 = (nr, SUBLANES, tq)
    q_pos = qb * tq + lax.broadcasted_iota(jnp.int32, (1, tq), 1)
    q_pos_b = jnp.broadcast_to(q_pos, (SUBLANES, tq))[None]
    key_id = (lax.broadcasted_iota(jnp.int32, shape3, 0) * SUBLANES
              + lax.broadcasted_iota(jnp.int32, shape3, 1))

    w = w_ref[0, 0]
    wb = [jnp.broadcast_to(w[h:h + 1, :], (SUBLANES, tq))[None] for h in range(IDX_HEADS)]

    def score_chunk(c, carry, diag):
        smax, smin = carry
        off = pl.multiple_of(c * ck, ck)
        kic = ki_ref[0, pl.ds(off, ck), :]
        sc = None
        for hg in range(IDX_HEADS // SCORE_HEADS_PER_DOT):
            l0 = hg * SCORE_HEADS_PER_DOT * tq
            lg = jnp.dot(kic, qi_ref[0, 0, :, l0:l0 + SCORE_HEADS_PER_DOT * tq], preferred_element_type=F32)
            for r in range(SCORE_HEADS_PER_DOT):
                t = jnp.maximum(lg[:, r * tq:(r + 1) * tq].reshape(shape3), 0.0) * wb[hg * SCORE_HEADS_PER_DOT + r]
                sc = t if sc is None else sc + t
        kid = key_id + off
        key = jnp.where(sc == 0.0, -2 - kid, _f32_to_key(sc))
        if diag:
            adm = kid <= q_pos_b
            smax = jnp.maximum(smax, jnp.max(jnp.where(adm, sc, -jnp.inf), axis=0))
            smin = jnp.minimum(smin, jnp.min(jnp.where(adm, sc, jnp.inf), axis=0))
            key = jnp.where(adm, key, INT32_MIN)
        else:
            smax = jnp.maximum(smax, jnp.max(sc, axis=0))
            smin = jnp.minimum(smin, jnp.min(sc, axis=0))
        sc_ref[c] = key.reshape(ck, tq)
        return smax, smin

    ext = (jnp.full((SUBLANES, tq), -jnp.inf, F32), jnp.full((SUBLANES, tq), jnp.inf, F32))
    ext = lax.fori_loop(0, n_chunks - 1, functools.partial(score_chunk, diag=False), ext)
    smax, smin = score_chunk(n_chunks - 1, ext, True)
    kmax = _f32_to_key(jnp.max(smax, axis=0, keepdims=True))
    kmin = _f32_to_key(jnp.min(smin, axis=0, keepdims=True))

    def count(pred):
        def body(c, acc):
            hits = pred(sc_ref[c].reshape(shape3), c).astype(jnp.int32)
            return acc + jnp.sum(hits.reshape(nr // COUNT_CHAINS, COUNT_CHAINS, SUBLANES, tq), axis=0)
        acc = lax.fori_loop(0, n_chunks, body, jnp.zeros((COUNT_CHAINS, SUBLANES, tq), jnp.int32))
        return jnp.sum(jnp.sum(acc, axis=0), axis=0, keepdims=True)

    def count_ge(kx):
        kx_b = jnp.broadcast_to(kx, (SUBLANES, tq))[None]
        return count(lambda x, c: x >= kx_b)

    n_adm = q_pos + 1
    log_k = math.log(topk - 0.5)
    c_pos = count_ge(jnp.zeros((1, tq), jnp.int32))
    pos = c_pos > topk
    lg_pos = jnp.log(jnp.maximum(c_pos, 1).astype(F32))
    frac1 = jnp.clip((lg_pos - log_k) / (lg_pos - math.log(0.5)), 0.0, 1.0)
    k1_pos = jnp.minimum(jnp.maximum(_f32_to_key(_key_to_f32(kmax + 1) * frac1), 1), kmax)
    k1 = jnp.where(pos, k1_pos, ZERO_BAND_LO)
    c1 = count_ge(k1)
    c_nn = c1
    band = jnp.logical_and(jnp.logical_and(c_pos < topk, c_nn > topk), jnp.logical_not(pos))
    all_in = n_adm <= topk
    done0 = jnp.logical_or(all_in, jnp.logical_or(c_pos == topk, c1 == topk))
    thr0 = jnp.where(all_in, INT32_MIN + 1, jnp.where(c_pos == topk, 0, k1))
    up1 = jnp.logical_and(pos, c1 > topk)
    down1 = jnp.logical_and(pos, c1 < topk)
    lo0 = jnp.where(band, -2 - n_chunks * ck, jnp.where(pos, jnp.where(up1, k1, 0), kmin))
    hi0 = jnp.where(band, -1, jnp.where(pos, jnp.where(down1, k1, kmax + 1), ZERO_BAND_LO))
    clo0 = jnp.where(band, c_nn, jnp.where(pos, jnp.where(up1, c1, c_pos), n_adm)).astype(F32)
    chi0 = jnp.where(band, c_pos, jnp.where(pos, jnp.where(down1, c1, 0), c_nn)).astype(F32)

    def as_value(key):
        return jnp.where(band, key.astype(F32), _key_to_f32(key))

    def search_step(st, how):
        lo, hi, c_lo, c_hi, done, thr = st
        span = hi - lo
        if how == "key_mid":
            kx = lo + lax.shift_right_logical(span, 1)
        else:
            if how == "value_mid":
                frac = 0.5
            else:
                lin = (c_lo - (topk - 0.5)) / (c_lo - c_hi)
                lg_lo = jnp.log(c_lo)
                lg = (lg_lo - log_k) / (lg_lo - jnp.log(jnp.maximum(c_hi, 0.5)))
                frac = jnp.clip(jnp.where(band, lin, lg), 0.0, 1.0)
            lo_v, hi_v = as_value(lo), as_value(hi)
            x = lo_v + (hi_v - lo_v) * frac
            kx = jnp.where(band, jnp.floor(x).astype(jnp.int32), _f32_to_key(x))
        kx = jnp.minimum(jnp.maximum(kx, lo + 1), hi - 1)
        cnt = count_ge(kx)
        live = jnp.logical_and(done == 0, span >= 2)
        hit = jnp.logical_and(live, cnt == topk)
        up = jnp.logical_and(live, cnt > topk)
        down = jnp.logical_and(live, cnt < topk)
        cf = cnt.astype(F32)
        return (jnp.where(up, kx, lo), jnp.where(down, kx, hi),
                jnp.where(up, cf, c_lo), jnp.where(down, cf, c_hi),
                jnp.where(hit, 1, done), jnp.where(hit, kx, thr))

    def unsettled(st):
        lo, hi, _, _, done, _ = st
        return jnp.max(jnp.where(jnp.logical_and(done == 0, hi - lo >= 2), 1, 0)) > 0

    def interp_cond(st):
        return jnp.logical_and(st[0] < interp_rounds, unsettled(st[1]))

    def interp_round(st):
        inner = search_step(search_step(search_step(st[1], "interp"), "interp"), "value_mid")
        return st[0] + 1, inner

    st = lax.while_loop(interp_cond, interp_round,
                        (jnp.int32(0), (lo0, hi0, clo0, chi0, done0.astype(jnp.int32), thr0)))[1]
    st = lax.while_loop(unsettled, lambda s: search_step(search_step(s, "key_mid"), "key_mid"), st)
    lo, _, _, c_gt, done, thr = st
    tied = done == 0
    thr = jnp.where(tied, lo, thr)
    thr_ref[...] = thr

    @pl.when(jnp.max(tied.astype(jnp.int32)) > 0)
    def _():
        want = topk - c_gt.astype(jnp.int32)
        thr_b = jnp.broadcast_to(thr, (SUBLANES, tq))[None]

        def bisect_index(_, carry):
            jlo, jhi = carry
            mid = jlo + lax.shift_right_logical(jhi - jlo, 1)
            mid_b = jnp.broadcast_to(mid, (SUBLANES, tq))[None]
            cnt = count(lambda x, c: jnp.logical_and(x == thr_b, key_id + c * ck <= mid_b))
            ge = cnt >= want
            return jnp.where(ge, jlo, mid), jnp.where(ge, mid, jhi)

        n_idx_steps = int(math.ceil(math.log2(sc_ref.shape[0] * ck + 1)))
        jinit = (jnp.full((1, tq), -1, jnp.int32), (n_chunks * ck - 1) + jnp.zeros((1, tq), jnp.int32))
        _, cut = lax.fori_loop(0, n_idx_steps, bisect_index, jinit)
        cut_b = jnp.broadcast_to(jnp.where(tied, cut, INT32_MAX), (SUBLANES, tq))[None]

        def drop(c, carry):
            x = sc_ref[c].reshape(shape3)
            gone = jnp.logical_and(x == thr_b, key_id + c * ck > cut_b)
            sc_ref[c] = jnp.where(gone, INT32_MIN, x).reshape(ck, tq)
            return carry

        lax.fori_loop(0, n_chunks, drop, 0)

    thr_b = jnp.broadcast_to(thr_ref[...], (SUBLANES, tq))[None]
    m_ref[...] = jnp.full(m_ref.shape, NEG_BIG, F32)
    acc_ref[...] = jnp.zeros(acc_ref.shape, F32)
    ak = min(ATTN_SUB_KEYS, ck)
    ones_rows = jnp.ones((acc_ref.shape[1] - HEAD_DIM, ak), BF16)
    vblocks = ck // LANES

    def attend_chunk(c, carry):
        off = pl.multiple_of(c * ck, ck)
        for sub in range(ck // ak):
            r0 = sub * ak
            sel = sc_ref[c, r0:r0 + ak, :].reshape(ak // SUBLANES, SUBLANES, tq) >= thr_b
            bias = jnp.where(sel, 0.0, NEG_BIG).reshape(ak, tq)
            for g in range(ATTN_KV_HEADS):
                kc = k_ref[0, g, pl.ds(off + r0, ak), :]
                vt = jnp.concatenate([v_ref[0, g, c * vblocks + sub * (ak // LANES) + i]
                                      for i in range(ak // LANES)], axis=1)
                vext = jnp.concatenate([vt, ones_rows], axis=0)
                for hp in range(ATTN_REP // HEADS_PER_DOT):
                    l0, l1 = hp * HEADS_PER_DOT * tq, (hp + 1) * HEADS_PER_DOT * tq
                    s = jnp.dot(kc, q_ref[0, 0, g, :, l0:l1], preferred_element_type=F32)
                    ps, alphas = [], []
                    for r in range(HEADS_PER_DOT):
                        h = g * ATTN_REP + hp * HEADS_PER_DOT + r
                        sh = s[:, r * tq:(r + 1) * tq] + bias
                        m_old = m_ref[h]
                        m_new = jnp.maximum(m_old, jnp.max(
                            jnp.max(sh.reshape(ak // SUBLANES, SUBLANES, tq), axis=0), axis=0, keepdims=True))
                        ps.append(jnp.exp2(sh - m_new).astype(BF16))
                        alphas.append(jnp.exp2(m_old - m_new))
                        m_ref[h] = m_new
                    acc_ref[g, :, l0:l1] = (jnp.concatenate(alphas, axis=1) * acc_ref[g, :, l0:l1]
                                            + jnp.dot(vext, jnp.concatenate(ps, axis=1),
                                                      preferred_element_type=F32))
        return carry

    lax.fori_loop(0, n_chunks, attend_chunk, 0)

    outs = []
    for g in range(ATTN_KV_HEADS):
        a = acc_ref[g]
        o = a[0:HEAD_DIM] / a[HEAD_DIM:HEAD_DIM + 1]
        outs += [o[:, r * tq:(r + 1) * tq] for r in range(ATTN_REP)]
    o_ref[0] = jnp.concatenate(outs, axis=0).T.astype(o_ref.dtype)


def _dsa(qt, kh, vt, qit, kir, wt, tq, ck):
    b, _, s, _ = kh.shape
    assert ck % tq == 0, "only the last causal chunk may hold inadmissible keys"
    topk = min(TOPK_MAX, s // 4)
    kernel = functools.partial(_dsa_kernel, tq=tq, ck=ck, topk=topk, interp_rounds=7)
    return pl.pallas_call(
        kernel,
        grid=(b, s // tq),
        in_specs=[pl.BlockSpec((1, 1, IDX_DIM, IDX_HEADS * tq), lambda bi, j: (bi, j, 0, 0)),
                  pl.BlockSpec((1, 1, IDX_HEADS, tq), lambda bi, j: (bi, j, 0, 0)),
                  pl.BlockSpec((1, 1, ATTN_KV_HEADS, HEAD_DIM, ATTN_REP * tq), lambda bi, j: (bi, j, 0, 0, 0)),
                  pl.BlockSpec((1, s, IDX_DIM), lambda bi, j: (bi, 0, 0)),
                  pl.BlockSpec((1, ATTN_KV_HEADS, s, HEAD_DIM), lambda bi, j: (bi, 0, 0, 0)),
                  pl.BlockSpec((1, ATTN_KV_HEADS, s // LANES, HEAD_DIM, LANES),
                               lambda bi, j: (bi, 0, 0, 0, 0))],
        out_specs=pl.BlockSpec((1, tq, ATTN_WIDTH), lambda bi, j: (bi, j, 0)),
        out_shape=jax.ShapeDtypeStruct((b, s, ATTN_WIDTH), BF16),
        scratch_shapes=[pltpu.VMEM((s // ck, ck, tq), jnp.int32),
                        pltpu.VMEM((1, tq), jnp.int32),
                        pltpu.VMEM((ATTN_HEADS, 1, tq), F32),
                        pltpu.VMEM((ATTN_KV_HEADS, HEAD_DIM + 2 * SUBLANES, ATTN_REP * tq), F32)],
        compiler_params=_cparams("parallel", "arbitrary"),
        name="dsa_mixer",
    )(qit, wt, qt, kir, kh, vt)


def _gelu_tanh(y):
    return 0.5 * y * (1.0 + jnp.tanh(math.sqrt(2.0 / math.pi) * (y + 0.044715 * (y * y * y))))


def _s5_kernel(u_ref, wb_ref, ar_ref, ai_ref, wc_ref, d_ref, gw_ref, gb_ref, o_ref,
               x_ref, st_ref, *, tt, nb):
    @pl.when(pl.program_id(0) == 0)
    def _():
        st_ref[...] = jnp.zeros(st_ref.shape, F32)

    u = u_ref[...]
    x_ref[...] = jnp.dot(u.astype(BF16), wb_ref[...], preferred_element_type=F32)
    ar = jnp.broadcast_to(ar_ref[...], (nb, S5_LANES))
    ai = jnp.broadcast_to(ai_ref[...], (nb, S5_LANES))

    def step(t, carry):
        xr, xi = carry
        r0 = pl.multiple_of(t * nb, nb)
        nxr = ar * xr - ai * xi + x_ref[pl.ds(r0, nb), 0:S5_LANES]
        nxi = ar * xi + ai * xr + x_ref[pl.ds(r0, nb), S5_LANES:2 * S5_LANES]
        x_ref[pl.ds(r0, nb), 0:S5_LANES] = nxr
        x_ref[pl.ds(r0, nb), S5_LANES:2 * S5_LANES] = nxi
        return nxr, nxi

    xr, xi = lax.fori_loop(0, tt, step, (st_ref[0:nb, :], st_ref[nb:2 * nb, :]), unroll=4)
    st_ref[0:nb, :] = xr
    st_ref[nb:2 * nb, :] = xi

    y = jnp.dot(x_ref[...].astype(BF16), wc_ref[...], preferred_element_type=F32)
    y = _gelu_tanh(y + d_ref[...] * u)
    gate = jnp.dot(y.astype(BF16), gw_ref[...], preferred_element_type=F32) + gb_ref[...]
    o_ref[...] = (y * jax.nn.sigmoid(gate)).astype(o_ref.dtype)


def _s5(u_t, wb, ar, ai, wc, d, gw, gb, nb, tt):
    n = u_t.shape[0]
    rows = nb * tt
    kernel = functools.partial(_s5_kernel, tt=tt, nb=nb)
    return pl.pallas_call(
        kernel,
        grid=(n // rows,),
        in_specs=[pl.BlockSpec((rows, S5_WIDTH), lambda i: (i, 0)),
                  _const_spec(wb.shape), _const_spec(ar.shape), _const_spec(ai.shape),
                  _const_spec(wc.shape), _const_spec(d.shape), _const_spec(gw.shape),
                  _const_spec(gb.shape)],
        out_specs=pl.BlockSpec((rows, S5_WIDTH), lambda i: (i, 0)),
        out_shape=jax.ShapeDtypeStruct((n, S5_WIDTH), BF16),
        scratch_shapes=[pltpu.VMEM((rows, 2 * S5_LANES), F32),
                        pltpu.VMEM((2 * nb, S5_LANES), F32)],
        compiler_params=_cparams("arbitrary"),
        name="s5_mixer",
    )(u_t, wb, ar, ai, wc, d, gw, gb)


_NT = (((1,), (1,)), ((), ()))
_TN = (((0,), (0,)), ((), ()))
_HI = lax.Precision.HIGHEST


def _ssd_kernel(z_ref, cw_ref, cb_ref, dtb_ref, a_ref, dsk_ref, ng_ref, tri_ref, trit_ref,
                o_ref, ext_ref, st_ref, *, cl):
    @pl.when(pl.program_id(1) == 0)
    def _():
        st_ref[...] = jnp.zeros(st_ref.shape, F32)
        ext_ref[0:SUBLANES, :] = jnp.zeros((SUBLANES, SSD_XBC), F32)

    tile = z_ref[0]
    zg = tile[:, 0:SSD_WIDTH]
    ext_ref[SUBLANES:SUBLANES + cl, :] = tile[:, SSD_WIDTH:SSD_WIDTH + SSD_XBC]
    conv = cb_ref[...]
    for k in range(SSD_CONV):
        start = SUBLANES - (SSD_CONV - 1) + k
        conv = conv + cw_ref[k:k + 1, :] * ext_ref[start:start + cl, :]
    ext_ref[0:SUBLANES, :] = ext_ref[cl:cl + SUBLANES, :]
    xa = conv * jax.nn.sigmoid(conv)
    xs = xa[:, 0:SSD_WIDTH]
    bm = xa[:, SSD_WIDTH:SSD_WIDTH + SSD_NGROUPS * SSD_STATE]
    cm = xa[:, SSD_WIDTH + SSD_NGROUPS * SSD_STATE:SSD_XBC]

    dtx = tile[:, SSD_WIDTH + SSD_XBC:SSD_SLAB] + dtb_ref[...]
    dt = jnp.maximum(dtx, 0.0) + jnp.log1p(jnp.exp(-jnp.abs(dtx)))
    a_dt = dt * a_ref[...]
    acs = jnp.dot(tri_ref[...], a_dt, precision=_HI, preferred_element_type=F32)
    acs_row = jnp.dot(a_dt.T[0:SUBLANES], trit_ref[...], precision=_HI, preferred_element_type=F32)
    causal = (lax.broadcasted_iota(jnp.int32, (cl, cl), 0)
              >= lax.broadcasted_iota(jnp.int32, (cl, cl), 1))

    cb_scores = []
    for g in range(SSD_NGROUPS):
        bg = bm[:, g * SSD_STATE:(g + 1) * SSD_STATE].astype(BF16)
        cg = cm[:, g * SSD_STATE:(g + 1) * SSD_STATE].astype(BF16)
        cb_scores.append(lax.dot_general(cg, bg, _NT, preferred_element_type=F32))

    rep = SSD_HEADS // SSD_NGROUPS
    ys = []
    for h in range(SSD_HEADS):
        g = h // rep
        acol = acs[:, h:h + 1]
        arow = acs_row[h:h + 1, :]
        alast = acs[cl - 1:cl, h:h + 1]
        decay = jnp.exp(jnp.where(causal, acol - arow, -jnp.inf))
        xs_h = xs[:, h * SSD_HEAD_DIM:(h + 1) * SSD_HEAD_DIM]
        xdt = (xs_h * dt[:, h:h + 1]).astype(BF16)
        y = jnp.dot((cb_scores[g] * decay).astype(BF16), xdt, preferred_element_type=F32)
        bg = bm[:, g * SSD_STATE:(g + 1) * SSD_STATE]
        cg = cm[:, g * SSD_STATE:(g + 1) * SSD_STATE].astype(BF16)
        prev = st_ref[h]
        y = y + jnp.dot(cg, prev.astype(BF16), preferred_element_type=F32) * jnp.exp(acol)
        bdec = (bg * jnp.exp(alast - acol)).astype(BF16)
        st_ref[h] = prev * jnp.exp(alast) + lax.dot_general(bdec, xdt, _TN, preferred_element_type=F32)
        ys.append(y + dsk_ref[:, h:h + 1] * xs_h)

    y = jnp.concatenate(ys, axis=-1) * (zg * jax.nn.sigmoid(zg))
    gw = SSD_WIDTH // SSD_NGROUPS
    outs = []
    for g in range(SSD_NGROUPS):
        yg = y[:, g * gw:(g + 1) * gw]
        outs.append(yg * lax.rsqrt(jnp.mean(yg * yg, axis=-1, keepdims=True) + EPS))
    o_ref[0] = (jnp.concatenate(outs, axis=-1) * ng_ref[...]).astype(o_ref.dtype)


def _ssd(z3, cw, cb, dtb, a, dsk, ng, tri, trit, cl):
    b, s, _ = z3.shape
    kernel = functools.partial(_ssd_kernel, cl=cl)
    return pl.pallas_call(
        kernel,
        grid=(b, s // cl),
        in_specs=[pl.BlockSpec((1, cl, SSD_SLAB), lambda bi, j: (bi, j, 0)),
                  _const_spec(cw.shape), _const_spec(cb.shape), _const_spec(dtb.shape),
                  _const_spec(a.shape), _const_spec(dsk.shape), _const_spec(ng.shape),
                  _const_spec(tri.shape), _const_spec(trit.shape)],
        out_specs=pl.BlockSpec((1, cl, SSD_WIDTH), lambda bi, j: (bi, j, 0)),
        out_shape=jax.ShapeDtypeStruct((b, s, SSD_WIDTH), BF16),
        scratch_shapes=[pltpu.VMEM((cl + 2 * SUBLANES, SSD_XBC), F32),
                        pltpu.VMEM((SSD_HEADS, SSD_STATE, SSD_HEAD_DIM), F32)],
        compiler_params=_cparams("parallel", "arbitrary"),
        name="ssd_mixer",
    )(z3, cw, cb, dtb, a, dsk, ng, tri, trit)


def _out_mlp_kernel(x_ref, a_ref, s_ref, m_ref, wo_ref, g_ref, wu_ref, wd_ref, o_ref, *, fc):
    wo = wo_ref
    mix = jnp.dot(a_ref[...], wo[0:ATTN_WIDTH, :], preferred_element_type=F32)
    mix = mix + jnp.dot(s_ref[...], wo[ATTN_WIDTH:ATTN_WIDTH + S5_WIDTH, :], preferred_element_type=F32)
    mix = mix + jnp.dot(m_ref[...], wo[ATTN_WIDTH + S5_WIDTH:, :], preferred_element_type=F32)
    x1 = x_ref[...] + mix
    ms = jnp.mean(x1 * x1, axis=-1, keepdims=True)
    h = (x1 * lax.rsqrt(ms + EPS) * g_ref[...]).astype(BF16)
    acc = x1
    for c in range(wu_ref.shape[1] // fc):
        up = jnp.maximum(jnp.dot(h, wu_ref[:, c * fc:(c + 1) * fc], preferred_element_type=F32), 0.0)
        acc = acc + jnp.dot((up * up).astype(BF16), wd_ref[c * fc:(c + 1) * fc, :],
                            preferred_element_type=F32)
    o_ref[...] = acc


def _out_mlp(x2d, attn, s5o, ssdo, wo, g, wu, wd, tm, fc):
    n, d = x2d.shape

    def row_spec(width):
        return pl.BlockSpec((tm, width), lambda i: (i, 0))

    def resident(shape):
        return pl.BlockSpec(shape, lambda i: (0, 0), pipeline_mode=pl.Buffered(1))

    return pl.pallas_call(
        functools.partial(_out_mlp_kernel, fc=fc),
        grid=(n // tm,),
        in_specs=[row_spec(d), row_spec(ATTN_WIDTH), row_spec(S5_WIDTH), row_spec(SSD_WIDTH),
                  resident(wo.shape), resident(g.shape), resident(wu.shape), resident(wd.shape)],
        out_specs=row_spec(d),
        out_shape=jax.ShapeDtypeStruct((n, d), F32),
        compiler_params=_cparams("parallel"),
        name="out_proj_mlp",
    )(x2d, attn, s5o, ssdo, wo, g, wu, wd)


def _pad_w_in(w_in_l):
    d = w_in_l.shape[0]
    sizes = (ATTN_WIDTH, KV_WIDTH, KV_WIDTH, IDX_HEADS * IDX_DIM, IDX_DIM, IDX_HEADS,
             S5_WIDTH, SSD_WIDTH, SSD_XBC, SSD_HEADS)
    offs = np.concatenate([[0], np.cumsum(sizes)])
    dst = (OFF_Q, OFF_K, OFF_V, OFF_QI, OFF_KW, OFF_KW + IDX_DIM,
           OFF_S5, OFF_SSD, OFF_SSD + SSD_WIDTH, OFF_SSD + SSD_WIDTH + SSD_XBC)
    pieces, col = [], 0
    for i in np.argsort(dst):
        pieces.append(jnp.zeros((d, dst[i] - col), w_in_l.dtype))
        pieces.append(w_in_l[:, offs[i]:offs[i] + sizes[i]])
        col = dst[i] + sizes[i]
    pieces.append(jnp.zeros((d, N_IN_PAD - col), w_in_l.dtype))
    return jnp.concatenate(pieces, axis=1).astype(BF16)


def _rope_tables(s):
    pos = jnp.arange(s, dtype=F32)
    inv_freq = ROPE_THETA ** (-jnp.arange(0, ROPE_DIM, 2, dtype=F32) / ROPE_DIM)
    ang = pos[:, None] * inv_freq[None, :]
    cos, sin = jnp.cos(ang), jnp.sin(ang)
    zeros = jnp.zeros((s, ROPE_HALF), F32)
    rest = HEAD_DIM - ROPE_DIM
    c = jnp.concatenate([cos, cos, jnp.ones((s, rest), F32)], axis=-1)
    sa = jnp.concatenate([-sin, zeros, jnp.zeros((s, rest), F32)], axis=-1)
    sb = jnp.concatenate([zeros, sin, jnp.zeros((s, rest), F32)], axis=-1)
    return c, sa, sb


def _block_diag_mean(width, seg, active=None):
    idx = np.arange(width)
    m = (idx[:, None] // seg == idx[None, :] // seg).astype(np.float32) / seg
    if active is not None:
        m = m * (idx[:, None] < active) * (idx[None, :] < active)
    return jnp.asarray(m, BF16)


def _pad_lanes(v, fill=0.0):
    v = v.reshape(1, -1).astype(F32)
    return jnp.pad(v, ((0, 0), (0, LANES - v.shape[1])), constant_values=fill)


def _s5_params(lam_re, lam_im, log_step, b_re, b_im, c_re, c_im):
    step = jnp.exp(log_step.astype(F32))[:, None]
    lr, li = lam_re.astype(F32), lam_im.astype(F32)
    mag = jnp.exp(lr * step)
    ab_re = mag * jnp.cos(li * step)
    ab_im = mag * jnp.sin(li * step)
    den = lr * lr + li * li
    cr = ((ab_re - 1.0) * lr + ab_im * li) / den
    ci = (ab_im * lr - (ab_re - 1.0) * li) / den
    bb_re = cr[..., None] * b_re - ci[..., None] * b_im
    bb_im = cr[..., None] * b_im + ci[..., None] * b_re
    eye = jnp.eye(S5_GROUPS, dtype=F32)

    def in_bd(bb):
        return jnp.einsum('gph,gk->ghkp', bb, eye).reshape(S5_WIDTH, S5_LANES)

    def out_bd(cc):
        return jnp.einsum('ghp,gk->gpkh', cc, eye).reshape(S5_LANES, S5_WIDTH)

    wb = jnp.concatenate([in_bd(bb_re), in_bd(bb_im)], axis=1).astype(BF16)
    wc = jnp.concatenate([out_bd(c_re.astype(F32)), -out_bd(c_im.astype(F32))], axis=0).astype(BF16)
    return wb, ab_re.reshape(1, S5_LANES), ab_im.reshape(1, S5_LANES), wc


def _tile(n, pref):
    return pref if n % pref == 0 else n


def kernel(x, norm_mix_g, w_in, attn_q_norm_g, attn_k_norm_g, idx_k_norm_g, s5_lambda_re, s5_lambda_im, s5_log_step, s5_b_re, s5_b_im, s5_c_re, s5_c_im, s5_d, s5_glu_w, s5_glu_b, ssd_conv_w, ssd_conv_b, ssd_dt_bias, ssd_a_log, ssd_d, ssd_norm_g, w_out, norm_mlp_g, w_up, w_down):
    b, s, d = x.shape
    n = b * s
    depth = w_in.shape[0]
    assert b == SUBLANES, "the S5 scan keeps one sequence per sublane"
    tm = _tile(n, 1024)
    tt = _tile(s, 1024)
    tq = _tile(s, 512)
    ck = _tile(s, 512)
    cl = _tile(s, 256)
    ts5 = _tile(s, 128)

    tables = _rope_tables(s)
    bdq = _block_diag_mean(ATTN_WIDTH, HEAD_DIM)
    bdk = _block_diag_mean(KV_WIDTH, HEAD_DIM)
    bdi = _block_diag_mean(LANES, IDX_DIM, active=IDX_DIM)
    tri = jnp.asarray(np.tril(np.ones((cl, cl), np.float32)))
    trit = jnp.asarray(np.triu(np.ones((cl, cl), np.float32)))

    x2d = x.reshape(n, d)
    for l in range(depth):
        qg = jnp.tile(attn_q_norm_g[l].astype(F32), ATTN_HEADS).reshape(1, ATTN_WIDTH)
        kg = jnp.tile(attn_k_norm_g[l].astype(F32), ATTN_KV_HEADS).reshape(1, KV_WIDTH)
        kig = _pad_lanes(idx_k_norm_g[l])
        z3, qt, kh, vt, qit, kir, wt = _in_proj(
            x2d.reshape(b, s, d), norm_mix_g[l].reshape(1, d).astype(F32), _pad_w_in(w_in[l]),
            tables, qg, kg, kig, bdq, bdk, bdi, tt, tq)
        attn = _dsa(qt, kh, vt, qit, kir, wt, tq, ck)

        wb, ar, ai, wc = _s5_params(s5_lambda_re[l], s5_lambda_im[l], s5_log_step[l],
                                    s5_b_re[l], s5_b_im[l], s5_c_re[l], s5_c_im[l])
        u_t = z3[:, :, OFF_S5 - ATTN_SLAB:].transpose(1, 0, 2).reshape(n, S5_WIDTH)
        s5_t = _s5(u_t, wb, ar, ai, wc, s5_d[l].reshape(1, S5_WIDTH).astype(F32),
                   s5_glu_w[l].astype(BF16), s5_glu_b[l].reshape(1, S5_WIDTH).astype(F32), b, ts5)
        s5o = s5_t.reshape(s, b, S5_WIDTH).transpose(1, 0, 2).reshape(n, S5_WIDTH)

        ssdo = _ssd(z3, ssd_conv_w[l].astype(F32), ssd_conv_b[l].reshape(1, SSD_XBC).astype(F32),
                    _pad_lanes(ssd_dt_bias[l]), _pad_lanes(-jnp.exp(ssd_a_log[l].astype(F32))),
                    _pad_lanes(ssd_d[l]), ssd_norm_g[l].reshape(1, SSD_WIDTH).astype(F32),
                    tri, trit, cl)

        x2d = _out_mlp(x2d, attn.reshape(n, ATTN_WIDTH), s5o, ssdo.reshape(n, SSD_WIDTH),
                       w_out[l].astype(BF16), norm_mlp_g[l].reshape(1, d).astype(F32),
                       w_up[l].astype(BF16), w_down[l].astype(BF16), tm, 1024)
    return x2d.reshape(b, s, d)
```
